```python
import math
import jax, jax.numpy as jnp
from jax import lax
import numpy as np

D_MODEL = 1024
BATCH = 8
SEQ = 2048
DEPTH = 4
DEC_BATCH = 128
DEC_SEQ = 1
PAST_LEN = 16384
PAGE_SIZE = 128

D_MIX = D_MODEL
W_A = D_MIX // 2
RG_BLOCKS = 8
RG_BW = W_A // RG_BLOCKS
CONV_W = 4
RG_C = 8.0
W_B = D_MIX - W_A
HG_HEADS = 4
HG_DK = W_B // HG_HEADS
HG_DV = W_B // HG_HEADS
HK = HG_HEADS * HG_DK
HG_CHUNK = 64
D_IN = 2 * W_A + 2 * HK + 2 * W_B
D_FF = ((8 * D_MODEL // 3 + 255) // 256) * 256
N_EXPERTS = 8
TOP_K = 2
N_DENSE = (DEPTH + 1) // 2
N_MOE = DEPTH // 2
EPS = 1e-6

kernel_name = "hymba_style_rglru_hgrn2_moe_adaln_decode"

F32 = jnp.float32


def rmsnorm(x, g):
    xf = x.astype(F32)
    y = xf * lax.rsqrt(jnp.mean(xf * xf, axis=-1, keepdims=True) + EPS)
    return y * g.astype(F32)


def causal_conv(x, buf, w, b):
    T = x.shape[1]
    xp = jnp.concatenate([buf.astype(x.dtype), x], axis=1)
    y = b.astype(F32)
    for j in range(CONV_W):
        y = y + xp[:, j:j + T].astype(F32) * w[j].astype(F32)
    return y, xp[:, -(CONV_W - 1):]


def rglru(xc, h0, pos0, wa, ba, wx, bx, lam):
    Bn, T, _ = xc.shape
    xb = xc.reshape(Bn, T, RG_BLOCKS, RG_BW)
    r = jax.nn.sigmoid(jnp.einsum('btnc,ncd->btnd', xb, wa.astype(F32)).reshape(Bn, T, W_A) + ba.astype(F32))
    i = jax.nn.sigmoid(jnp.einsum('btnc,ncd->btnd', xb, wx.astype(F32)).reshape(Bn, T, W_A) + bx.astype(F32))
    log_a = -RG_C * jax.nn.softplus(-lam.astype(F32)) * r
    reset = ((jnp.arange(T) + pos0) == 0)[None, :, None]
    a = jnp.where(reset, 0.0, jnp.exp(log_a))
    mult = jnp.where(reset, 1.0, jnp.sqrt(-jnp.expm1(2.0 * log_a)))
    b = mult * (i * xc)
    b = b.at[:, 0].add(a[:, 0] * h0.astype(F32))

    def combine(lft, rgt):
        a1, b1 = lft
        a2, b2 = rgt
        return a1 * a2, a2 * b1 + b2

    _, h = lax.associative_scan(combine, (a, b), axis=1)
    return h, h[:, -1]


def hgrn2_chunked(q, k, v, logf, S0):
    Bn, T, H, _ = q.shape
    C = math.gcd(T, HG_CHUNK)
    NC = T // C

    def to_chunks(z):
        return z.reshape(Bn, NC, C, H, z.shape[-1]).transpose(1, 0, 3, 2, 4)

    mask = jnp.tril(jnp.ones((C, C), dtype=bool))[None, None, :, :, None]

    def step(S, inp):
        qc, kc, vc, gc = inp
        bcum = jnp.cumsum(gc, axis=2)
        o_inter = jnp.einsum('bhcd,bhde->bhce', qc * jnp.exp(bcum), S)
        diff = bcum[:, :, :, None, :] - bcum[:, :, None, :, :]
        decay = jnp.exp(jnp.where(mask, diff, -jnp.inf))
        A = jnp.einsum('bhtd,bhsd,bhtsd->bhts', qc, kc, decay)
        o_intra = jnp.einsum('bhts,bhse->bhte', A, vc)
        b_last = bcum[:, :, -1:, :]
        S_new = jnp.exp(b_last[:, :, 0, :])[..., None] * S + jnp.einsum(
            'bhsd,bhse->bhde', kc * jnp.exp(b_last - bcum), vc)
        return S_new, o_inter + o_intra

    S_fin, o = lax.scan(step, S0, (to_chunks(q), to_chunks(k), to_chunks(v), to_chunks(logf)))
    o = o.transpose(1, 0, 3, 2, 4).reshape(Bn, T, H, v.shape[-1])
    return o, S_fin


def swiglu(h, w1, w3, w2):
    return (jax.nn.silu(h @ w1) * (h @ w3)) @ w2


def moe_swiglu(h, router, w1, w3, w2):
    logits = (h @ router).astype(F32)
    topv, topi = lax.top_k(logits, TOP_K)
    wts = jax.nn.softmax(topv, axis=-1)
    gates = jnp.sum(jax.nn.one_hot(topi, N_EXPERTS, dtype=F32) * wts[..., None], axis=-2)
    out = jnp.zeros(h.shape[:-1] + (D_MODEL,), F32)
    for e in range(N_EXPERTS):
        out = out + gates[..., e:e + 1] * swiglu(h, w1[e], w3[e], w2[e]).astype(F32)
    return out


def setup_inputs(seed: int = 0) -> dict:
    key = jax.random.key(seed)
    ks = iter(jax.random.split(key, 40))

    def nrm(shape, scale=1.0):
        return jax.random.normal(next(ks), shape, F32) * scale

    u = jax.random.uniform(next(ks), (DEPTH, W_A), F32, minval=0.9, maxval=0.999)
    s = u ** (1.0 / RG_C)
    rg_lambda = jnp.log(s) - jnp.log1p(-s)
    return {
        "x_prompt": nrm((BATCH, SEQ, D_MODEL)),
        "x_sample": nrm((DEC_BATCH, DEC_SEQ, D_MODEL)),
        "c_prompt": nrm((BATCH, D_MODEL)),
        "c_sample": nrm((DEC_BATCH, D_MODEL)),
        "state_rglru_h": nrm((DEPTH, DEC_BATCH, W_A)),
        "state_rglru_conv": nrm((DEPTH, DEC_BATCH, CONV_W - 1, W_A)),
        "state_hgrn": nrm((DEPTH, DEC_BATCH, HG_HEADS, HG_DK, HG_DV), 0.5),
        "mod_w": nrm((DEPTH, D_MODEL, 6 * D_MODEL), 0.5 * D_MODEL ** -0.5),
        "mod_b": nrm((DEPTH, 6 * D_MODEL), 0.02),
        "norm1_g": 1.0 + nrm((DEPTH, D_MODEL), 0.02),
        "norm2_g": 1.0 + nrm((DEPTH, D_MODEL), 0.02),
        "w_in": nrm((DEPTH, D_MODEL, D_IN), D_MODEL ** -0.5),
        "conv_w": nrm((DEPTH, CONV_W, W_A), CONV_W ** -0.5),
        "conv_b": nrm((DEPTH, W_A), 0.02),
        "rg_wa": nrm((DEPTH, RG_BLOCKS, RG_BW, RG_BW), RG_BW ** -0.5),
        "rg_ba": nrm((DEPTH, W_A), 0.02),
        "rg_wx": nrm((DEPTH, RG_BLOCKS, RG_BW, RG_BW), RG_BW ** -0.5),
        "rg_bx": nrm((DEPTH, W_A), 0.02),
        "rg_lambda": rg_lambda,
        "rg_out_g": 1.0 + nrm((DEPTH, W_A), 0.02),
        "hg_lb_logits": nrm((DEPTH, HK), 0.1),
        "hg_norm_g": 1.0 + nrm((DEPTH, W_B), 0.02),
        "w_out": nrm((DEPTH, D_MIX, D_MODEL), D_MIX ** -0.5),
        "ffn_w1": nrm((N_DENSE, D_MODEL, D_FF), D_MODEL ** -0.5),
        "ffn_w3": nrm((N_DENSE, D_MODEL, D_FF), D_MODEL ** -0.5),
        "ffn_w2": nrm((N_DENSE, D_FF, D_MODEL), D_FF ** -0.5),
        "router_w": nrm((N_MOE, D_MODEL, N_EXPERTS), D_MODEL ** -0.5),
        "moe_w1": nrm((N_MOE, N_EXPERTS, D_MODEL, D_FF), D_MODEL ** -0.5),
        "moe_w3": nrm((N_MOE, N_EXPERTS, D_MODEL, D_FF), D_MODEL ** -0.5),
        "moe_w2": nrm((N_MOE, N_EXPERTS, D_FF, D_MODEL), D_FF ** -0.5),
        "final_g": 1.0 + nrm((D_MODEL,), 0.02),
    }


def reference(x_prompt, x_sample, c_prompt, c_sample, state_rglru_h, state_rglru_conv, state_hgrn,
              mod_w, mod_b, norm1_g, norm2_g, w_in, conv_w, conv_b, rg_wa, rg_ba, rg_wx, rg_bx,
              rg_lambda, rg_out_g, hg_lb_logits, hg_norm_g, w_out, ffn_w1, ffn_w3, ffn_w2,
              router_w, moe_w1, moe_w3, moe_w2, final_g):
    p_lb = jax.nn.softmax(hg_lb_logits.astype(F32), axis=0)
    lower_bounds = jnp.cumsum(p_lb, axis=0) - p_lb[0]

    def layer(l, x, c, h0, conv0, S0, pos0):
        dt = x.dtype
        Bn, T, _ = x.shape
        mod = jax.nn.silu(c.astype(F32)) @ mod_w[l].astype(F32) + mod_b[l].astype(F32)
        sh1, sc1, gt1, sh2, sc2, gt2 = jnp.split(mod[:, None, :], 6, axis=-1)
        h = (rmsnorm(x, norm1_g[l]) * (1.0 + sc1) + sh1).astype(dt)
        u = h @ w_in[l]
        xa, ga, qb, fb, ib, gb = jnp.split(
            u, [W_A, 2 * W_A, 2 * W_A + HK, 2 * W_A + 2 * HK, 2 * W_A + 2 * HK + W_B], axis=-1)
        xc, convN = causal_conv(xa, conv0, conv_w[l], conv_b[l])
        hseq, hN = rglru(xc, h0, pos0, rg_wa[l], rg_ba[l], rg_wx[l], rg_bx[l], rg_lambda[l])
        yA = rmsnorm(hseq * jax.nn.gelu(ga.astype(F32)), rg_out_g[l])
        lb = lower_bounds[l]
        f = lb + (1.0 - lb) * jax.nn.sigmoid(fb.astype(F32))
        q = jax.nn.silu(qb.astype(F32)).reshape(Bn, T, HG_HEADS, HG_DK)
        k = (1.0 - f).reshape(Bn, T, HG_HEADS, HG_DK)
        logf = jnp.log(f).reshape(Bn, T, HG_HEADS, HG_DK)
        v = ib.astype(F32).reshape(Bn, T, HG_HEADS, HG_DV)
        o, SN = hgrn2_chunked(q, k, v, logf, S0.astype(F32))
        o = rmsnorm(o, hg_norm_g[l].reshape(HG_HEADS, HG_DV)) * jax.nn.silu(
            gb.astype(F32)).reshape(Bn, T, HG_HEADS, HG_DV)
        yB = o.reshape(Bn, T, W_B)
        y = jnp.concatenate([yA, yB], axis=-1).astype(dt) @ w_out[l]
        x = (x.astype(F32) + gt1 * y.astype(F32)).astype(dt)
        h2 = (rmsnorm(x, norm2_g[l]) * (1.0 + sc2) + sh2).astype(dt)
        if l % 2 == 0:
            fo = swiglu(h2, ffn_w1[l // 2], ffn_w3[l // 2], ffn_w2[l // 2]).astype(F32)
        else:
            fo = moe_swiglu(h2, router_w[l // 2], moe_w1[l // 2], moe_w3[l // 2], moe_w2[l // 2])
        x = (x.astype(F32) + gt2 * fo).astype(dt)
        return x, hN.astype(dt), convN.astype(dt), SN.astype(dt)

    xp, xs = x_prompt, x_sample
    dtp = x_prompt.dtype
    hp0 = jnp.zeros((BATCH, W_A), dtp)
    convp0 = jnp.zeros((BATCH, CONV_W - 1, W_A), dtp)
    Sp0 = jnp.zeros((BATCH, HG_HEADS, HG_DK, HG_DV), dtp)
    hp_l, convp_l, Sp_l, hs_l, convs_l, Ss_l = [], [], [], [], [], []
    for l in range(DEPTH):
        xp, hp, cp, sp = layer(l, xp, c_prompt, hp0, convp0, Sp0, 0)
        xs, hs, cs, ss = layer(l, xs, c_sample, state_rglru_h[l], state_rglru_conv[l], state_hgrn[l], PAST_LEN)
        hp_l.append(hp); convp_l.append(cp); Sp_l.append(sp)
        hs_l.append(hs); convs_l.append(cs); Ss_l.append(ss)
    y_prompt = rmsnorm(xp, final_g).astype(x_prompt.dtype)
    y_sample = rmsnorm(xs, final_g).astype(x_sample.dtype)
    return (y_prompt, y_sample,
            jnp.stack(hp_l), jnp.stack(convp_l), jnp.stack(Sp_l),
            jnp.stack(hs_l), jnp.stack(convs_l), jnp.stack(Ss_l))
```

```python
import functools

import jax
import jax.numpy as jnp
from jax import lax
from jax.experimental import pallas as pl
from jax.experimental.pallas import tpu as pltpu

F32 = jnp.float32
BF16 = jnp.bfloat16
HIGHEST = lax.Precision.HIGHEST

D_MODEL = 1024
DEPTH = 4
W_A = 512
RG_BLOCKS = 8
CONV_W = 4
RG_C = 8.0
W_B = 512
HG_HEADS = 4
HG_D = 128
D_IN = 3072
D_FF = 2816
N_EXPERTS = 8
EPS = 1e-6
PAST_LEN = 16384

LANES = 128
SUBLANES = 8
VMEM_LIMIT = 56 * 1024 * 1024

_P_CONV_W, _P_CONV_B, _P_BA, _P_BX, _P_LAM, _P_RG_G, _P_LB, _P_HG_G = 0, 4, 5, 6, 7, 8, 9, 10
_P_ROWS = 16


def _silu(x):
    return x * jax.nn.sigmoid(x)


def _gelu_tanh(x):
    cdf = 0.5 * (1.0 + jnp.tanh(0.7978845608028654 * (x + 0.044715 * (x * x * x))))
    return x * cdf


def _rmsnorm(x, g):
    return x * lax.rsqrt(jnp.mean(x * x, axis=-1, keepdims=True) + EPS) * g


def _softplus(z):
    return jnp.maximum(z, 0.0) + jnp.log1p(jnp.exp(-jnp.abs(z)))


def _bdot(a, b):
    return jnp.dot(a.astype(BF16), b.astype(BF16), preferred_element_type=F32)


def _rglru_coeffs(xc, wg, p_ref):
    gates = _bdot(xc, wg)
    r = jax.nn.sigmoid(gates[:, :W_A] + p_ref[_P_BA:_P_BA + 1, :])
    i = jax.nn.sigmoid(gates[:, W_A:] + p_ref[_P_BX:_P_BX + 1, :])
    log_a = (-RG_C * _softplus(-p_ref[_P_LAM:_P_LAM + 1, :])) * r
    a = jnp.exp(log_a)
    one_minus_a2 = -jnp.tanh(log_a) * (a * a + 1.0)
    return a, one_minus_a2, i * xc


def _head_rmsnorm_gate(o, gb, g_row):
    outs = []
    for h in range(HG_HEADS):
        hs = slice(h * HG_D, (h + 1) * HG_D)
        outs.append(_rmsnorm(o[:, hs], g_row[:, hs]))
    return jnp.concatenate(outs, axis=1) * _silu(gb)


def _mod_kernel(c_ref, w_ref, b_ref, o_ref):
    o_ref[...] = _bdot(_silu(c_ref[...]), w_ref[...]) + b_ref[...]


def _modulation(c_all, mod_w, mod_b):
    rows = c_all.shape[0]
    tn = 1536
    n_out = mod_w.shape[-1]
    return pl.pallas_call(
        _mod_kernel,
        grid=(DEPTH, n_out // tn),
        in_specs=[
            pl.BlockSpec((rows, D_MODEL), lambda l, j: (0, 0)),
            pl.BlockSpec((None, D_MODEL, tn), lambda l, j: (l, 0, j)),
            pl.BlockSpec((None, 1, tn), lambda l, j: (l, 0, j)),
        ],
        out_specs=pl.BlockSpec((None, rows, tn), lambda l, j: (l, 0, j)),
        out_shape=jax.ShapeDtypeStruct((DEPTH, rows, n_out), F32),
        compiler_params=pltpu.CompilerParams(
            dimension_semantics=("arbitrary", "arbitrary"), vmem_limit_bytes=VMEM_LIMIT),
        name="adaln_mod",
    )(c_all, mod_w, mod_b.reshape(DEPTH, 1, n_out))


def _lb_kernel(l_ref, o_ref):
    x = l_ref[...]
    e = jnp.exp(x - jnp.max(x, axis=0, keepdims=True))
    p = e / jnp.sum(e, axis=0, keepdims=True)
    p0 = p[0:1, :]
    run = p0
    o_ref[0:1, :] = run - p0
    for r in range(1, DEPTH):
        run = run + p[r:r + 1, :]
        o_ref[r:r + 1, :] = run - p0


def _lower_bounds(hg_lb_logits):
    return pl.pallas_call(
        _lb_kernel,
        out_shape=jax.ShapeDtypeStruct(hg_lb_logits.shape, F32),
        name="hgrn_lower_bounds",
    )(hg_lb_logits)


def _hgrn_chunk(q, k, v, logf, st):
    C = q.shape[0]
    rc = lax.broadcasted_iota(jnp.int32, (C, C), 0)
    cc = lax.broadcasted_iota(jnp.int32, (C, C), 1)
    tril = (rc >= cc).astype(F32)
    b = jnp.dot(tril, logf, precision=HIGHEST, preferred_element_type=F32)
    row = lax.broadcasted_iota(jnp.int32, (C, W_B), 0)
    xor = rc ^ cc

    heads = [slice(h * HG_D, (h + 1) * HG_D) for h in range(HG_HEADS)]
    nt = (((1,), (1,)), ((), ()))
    qb, kb = q.astype(BF16), k.astype(BF16)
    att = [jnp.where(rc == cc, lax.dot_general(qb[:, hs], kb[:, hs], nt, preferred_element_type=F32), 0.0)
           for hs in heads]

    c = b - logf
    d = b
    m, lg = 1, 0
    while m < C:
        odd = (row & m) != 0
        z = jnp.where(odd, b - c, d - b)
        x = (jnp.where(odd, q, k) * jnp.exp(z)).astype(BF16)
        mask = ((xor >> lg) == 1) & ((rc & m) != 0)
        for h, hs in enumerate(heads):
            p = lax.dot_general(x[:, hs], x[:, hs], nt, preferred_element_type=F32)
            att[h] = jnp.where(mask, p, att[h])
        c = jnp.where(odd, pltpu.roll(c, m, 0), c)
        d = jnp.where(odd, d, pltpu.roll(d, C - m, 0))
        m, lg = 2 * m, lg + 1

    b_last = b[C - 1:C, :]
    qe = (q * jnp.exp(b)).astype(BF16)
    kd = (k * jnp.exp(b_last - b)).astype(BF16)
    vb = v.astype(BF16)
    tn = (((0,), (0,)), ((), ()))
    outs, new_st = [], []
    for h, hs in enumerate(heads):
        o_inter = lax.dot_general(qe[:, hs], st[h].astype(BF16), nt, preferred_element_type=F32)
        o_intra = jnp.dot(att[h].astype(BF16), vb[:, hs], preferred_element_type=F32)
        outs.append(o_inter + o_intra)
        upd = lax.dot_general(vb[:, hs], kd[:, hs], tn, preferred_element_type=F32)
        new_st.append(jnp.exp(b_last[:, hs]) * st[h] + upd)
    return jnp.concatenate(outs, axis=1), new_st


def _shift_rows(x, k, fill):
    row = lax.broadcasted_iota(jnp.int32, x.shape, 0)
    return jnp.where(row >= k, pltpu.roll(x, k, 0), fill)


def _mix_prompt_kernel(x_ref, g1_ref, sh_ref, sc_ref, gt_ref, win_ref, p_ref, wg_ref, wout_ref,
                       xo_ref, hn_ref, tail_ref, sn_ref,
                       hcar, ext, st_scr, *, hg_chunk):
    t = pl.program_id(1)
    tc = x_ref.shape[0]

    @pl.when(t == 0)
    def _():
        hcar[...] = jnp.zeros_like(hcar)
        ext[0:SUBLANES, :] = jnp.zeros((SUBLANES, W_A), F32)
        st_scr[...] = jnp.zeros_like(st_scr)

    x = x_ref[...]
    h = _rmsnorm(x, g1_ref[...]) * (1.0 + sc_ref[...]) + sh_ref[...]
    u = _bdot(h, win_ref[...])

    xa = u[:, 0:W_A]
    ga = u[:, W_A:2 * W_A]
    ext[SUBLANES:, :] = xa
    xc = p_ref[_P_CONV_B:_P_CONV_B + 1, :]
    for j in range(CONV_W):
        lo = SUBLANES - (CONV_W - 1) + j
        xc = xc + ext[lo:lo + tc, :] * p_ref[_P_CONV_W + j:_P_CONV_W + j + 1, :]
    tail = xa[tc - SUBLANES:, :]
    ext[0:SUBLANES, :] = tail
    tail_ref[...] = tail

    a, one_minus_a2, ix = _rglru_coeffs(xc, wg_ref[...], p_ref)
    row = lax.broadcasted_iota(jnp.int32, (tc, W_A), 0)
    first = (row == 0) & (t == 0)
    a = jnp.where(first, 0.0, a)
    bb = jnp.where(first, 1.0, jnp.sqrt(one_minus_a2)) * ix
    k = 1
    while k < tc:
        bb = a * _shift_rows(bb, k, 0.0) + bb
        a = a * _shift_rows(a, k, 1.0)
        k *= 2
    hseq = a * hcar[...] + bb
    h_last = hseq[tc - 1:tc, :]
    hcar[...] = h_last
    hn_ref[...] = h_last
    y_a = _rmsnorm(hseq * _gelu_tanh(ga), p_ref[_P_RG_G:_P_RG_G + 1, :])

    lb = p_ref[_P_LB:_P_LB + 1, :]
    f = lb + (1.0 - lb) * jax.nn.sigmoid(u[:, 3 * W_A:4 * W_A])
    q = _silu(u[:, 2 * W_A:3 * W_A])
    kk = 1.0 - f
    logf = jnp.log(f)
    v = u[:, 4 * W_A:5 * W_A]
    st = [st_scr[hh] for hh in range(HG_HEADS)]
    outs = []
    for c0 in range(0, tc, hg_chunk):
        cs = slice(c0, c0 + hg_chunk)
        o_c, st = _hgrn_chunk(q[cs], kk[cs], v[cs], logf[cs], st)
        outs.append(o_c)
    o = outs[0] if len(outs) == 1 else jnp.concatenate(outs, axis=0)
    for hh in range(HG_HEADS):
        st_scr[hh] = st[hh]
        sn_ref[hh] = st[hh].T
    y_b = _head_rmsnorm_gate(o, u[:, 5 * W_A:6 * W_A], p_ref[_P_HG_G:_P_HG_G + 1, :])

    y = _bdot(jnp.concatenate([y_a, y_b], axis=1), wout_ref[...])
    xo_ref[...] = x + gt_ref[...] * y


def _mix_prompt(x, g1, mod, w_in, pvec, wg, w_out, *, tc=256, hg_chunk=128):
    bsz, seq, _ = x.shape
    const = lambda b, t: (0, 0)
    kern = functools.partial(_mix_prompt_kernel, hg_chunk=hg_chunk)
    return pl.pallas_call(
        kern,
        grid=(bsz, seq // tc),
        in_specs=[
            pl.BlockSpec((None, tc, D_MODEL), lambda b, t: (b, t, 0)),
            pl.BlockSpec((1, D_MODEL), const),
            pl.BlockSpec((None, 1, D_MODEL), lambda b, t: (b, 0, 0)),
            pl.BlockSpec((None, 1, D_MODEL), lambda b, t: (b, 0, 1)),
            pl.BlockSpec((None, 1, D_MODEL), lambda b, t: (b, 0, 2)),
            pl.BlockSpec((D_MODEL, D_IN), const),
            pl.BlockSpec((_P_ROWS, W_A), const),
            pl.BlockSpec((W_A, 2 * W_A), const),
            pl.BlockSpec((D_MODEL, D_MODEL), const),
        ],
        out_specs=[
            pl.BlockSpec((None, tc, D_MODEL), lambda b, t: (b, t, 0)),
            pl.BlockSpec((None, 1, W_A), lambda b, t: (b, 0, 0)),
            pl.BlockSpec((None, SUBLANES, W_A), lambda b, t: (b, 0, 0)),
            pl.BlockSpec((None, HG_HEADS, HG_D, HG_D), lambda b, t: (b, 0, 0, 0)),
        ],
        out_shape=[
            jax.ShapeDtypeStruct(x.shape, F32),
            jax.ShapeDtypeStruct((bsz, 1, W_A), F32),
            jax.ShapeDtypeStruct((bsz, SUBLANES, W_A), F32),
            jax.ShapeDtypeStruct((bsz, HG_HEADS, HG_D, HG_D), F32),
        ],
        scratch_shapes=[
            pltpu.VMEM((1, W_A), F32),
            pltpu.VMEM((tc + SUBLANES, W_A), F32),
            pltpu.VMEM((HG_HEADS, HG_D, HG_D), F32),
        ],
        compiler_params=pltpu.CompilerParams(
            dimension_semantics=("arbitrary", "arbitrary"), vmem_limit_bytes=VMEM_LIMIT),
        name="mix_prompt",
    )(x, g1, mod, mod, mod, w_in, pvec, wg, w_out)


def _in_sample_kernel(x_ref, g1_ref, sh_ref, sc_ref, win_ref, u_ref):
    h = _rmsnorm(x_ref[...], g1_ref[...]) * (1.0 + sc_ref[...]) + sh_ref[...]
    u_ref[...] = _bdot(h, win_ref[...])


def _in_sample(x, g1, mod, w_in):
    n = x.shape[0]
    tn = 1024
    return pl.pallas_call(
        _in_sample_kernel,
        grid=(D_IN // tn,),
        in_specs=[
            pl.BlockSpec((n, D_MODEL), lambda j: (0, 0)),
            pl.BlockSpec((1, D_MODEL), lambda j: (0, 0)),
            pl.BlockSpec((n, D_MODEL), lambda j: (0, 0)),
            pl.BlockSpec((n, D_MODEL), lambda j: (0, 1)),
            pl.BlockSpec((D_MODEL, tn), lambda j: (0, j)),
        ],
        out_specs=pl.BlockSpec((n, tn), lambda j: (0, j)),
        out_shape=jax.ShapeDtypeStruct((n, D_IN), F32),
        compiler_params=pltpu.CompilerParams(
            dimension_semantics=("arbitrary",), vmem_limit_bytes=VMEM_LIMIT),
        name="in_sample",
    )(x, g1, mod, mod, w_in)


def _mix_sample_kernel(u_ref, h0_ref, c0_ref, s0_ref, p_ref, wg_ref,
                       y_ref, hn_ref, cn_ref, sn_ref, o_scr):
    bs = u_ref.shape[0]
    u = u_ref[...]
    xa = u[:, 0:W_A]
    ga = u[:, W_A:2 * W_A]
    c0 = c0_ref[...]
    xc = p_ref[_P_CONV_B:_P_CONV_B + 1, :]
    for j in range(CONV_W - 1):
        xc = xc + c0[:, j * W_A:(j + 1) * W_A] * p_ref[_P_CONV_W + j:_P_CONV_W + j + 1, :]
    xc = xc + xa * p_ref[_P_CONV_W + CONV_W - 1:_P_CONV_W + CONV_W, :]
    cn_ref[:, 0:2 * W_A] = c0[:, W_A:]
    cn_ref[:, 2 * W_A:] = xa

    a, one_minus_a2, ix = _rglru_coeffs(xc, wg_ref[...], p_ref)
    hnew = jnp.sqrt(one_minus_a2) * ix + a * h0_ref[...]
    hn_ref[...] = hnew
    y_a = _rmsnorm(hnew * _gelu_tanh(ga), p_ref[_P_RG_G:_P_RG_G + 1, :])

    lb = p_ref[_P_LB:_P_LB + 1, :]
    f = lb + (1.0 - lb) * jax.nn.sigmoid(u[:, 3 * W_A:4 * W_A])
    q = _silu(u[:, 2 * W_A:3 * W_A])
    kk = 1.0 - f
    v = u[:, 4 * W_A:5 * W_A]
    eye = (lax.broadcasted_iota(jnp.int32, (bs, bs), 0) == lax.broadcasted_iota(jnp.int32, (bs, bs), 1)).astype(F32)
    cols = lax.dot_general(jnp.concatenate([f, kk, q], axis=1), eye, (((0,), (0,)), ((), ())),
                           precision=HIGHEST, preferred_element_type=F32)
    for j in range(bs):
        for hh in range(HG_HEADS):
            hs = slice(hh * HG_D, (hh + 1) * HG_D)
            fcol = cols[hh * HG_D:(hh + 1) * HG_D, j:j + 1]
            kcol = cols[W_B + hh * HG_D:W_B + (hh + 1) * HG_D, j:j + 1]
            qcol = cols[2 * W_B + hh * HG_D:2 * W_B + (hh + 1) * HG_D, j:j + 1]
            s_new = fcol * s0_ref[j, hh] + kcol * v[j:j + 1, hs]
            sn_ref[j, hh] = s_new
            o_scr[j:j + 1, hs] = jnp.sum(qcol * s_new, axis=0, keepdims=True)
    y_b = _head_rmsnorm_gate(o_scr[...], u[:, 5 * W_A:6 * W_A], p_ref[_P_HG_G:_P_HG_G + 1, :])
    y_ref[:, 0:W_A] = y_a
    y_ref[:, W_A:] = y_b


def _mix_sample(u, h0, c0, s0, pvec, wg, *, bs=8):
    n = u.shape[0]
    const = lambda i: (0, 0)
    return pl.pallas_call(
        _mix_sample_kernel,
        grid=(n // bs,),
        in_specs=[
            pl.BlockSpec((bs, D_IN), lambda i: (i, 0)),
            pl.BlockSpec((bs, W_A), lambda i: (i, 0)),
            pl.BlockSpec((bs, (CONV_W - 1) * W_A), lambda i: (i, 0)),
            pl.BlockSpec((bs, HG_HEADS, HG_D, HG_D), lambda i: (i, 0, 0, 0)),
            pl.BlockSpec((_P_ROWS, W_A), const),
            pl.BlockSpec((W_A, 2 * W_A), const),
        ],
        out_specs=[
            pl.BlockSpec((bs, D_MODEL), lambda i: (i, 0)),
            pl.BlockSpec((bs, W_A), lambda i: (i, 0)),
            pl.BlockSpec((bs, (CONV_W - 1) * W_A), lambda i: (i, 0)),
            pl.BlockSpec((bs, HG_HEADS, HG_D, HG_D), lambda i: (i, 0, 0, 0)),
        ],
        out_shape=[
            jax.ShapeDtypeStruct((n, D_MODEL), F32),
            jax.ShapeDtypeStruct((n, W_A), F32),
            jax.ShapeDtypeStruct((n, (CONV_W - 1) * W_A), F32),
            jax.ShapeDtypeStruct((n, HG_HEADS, HG_D, HG_D), F32),
        ],
        scratch_shapes=[pltpu.VMEM((bs, W_B), F32)],
        compiler_params=pltpu.CompilerParams(
            dimension_semantics=("arbitrary",), vmem_limit_bytes=VMEM_LIMIT),
        name="mix_sample",
    )(u, h0, c0, s0, pvec, wg)


def _out_sample_kernel(x_ref, y_ref, gt_ref, wout_ref, xo_ref):
    xo_ref[...] = x_ref[...] + gt_ref[...] * _bdot(y_ref[...], wout_ref[...])


def _out_sample(x, y, mod, w_out):
    n = x.shape[0]
    return pl.pallas_call(
        _out_sample_kernel,
        grid=(1,),
        in_specs=[
            pl.BlockSpec((n, D_MODEL), lambda i: (0, 0)),
            pl.BlockSpec((n, D_MODEL), lambda i: (0, 0)),
            pl.BlockSpec((n, D_MODEL), lambda i: (0, 2)),
            pl.BlockSpec((D_MODEL, D_MODEL), lambda i: (0, 0)),
        ],
        out_specs=pl.BlockSpec((n, D_MODEL), lambda i: (0, 0)),
        out_shape=jax.ShapeDtypeStruct((n, D_MODEL), F32),
        compiler_params=pltpu.CompilerParams(
            dimension_semantics=("arbitrary",), vmem_limit_bytes=VMEM_LIMIT),
        name="out_sample",
    )(x, y, mod, w_out)


def _top2_gates(h, router):
    logits = jnp.dot(h, router, precision=HIGHEST, preferred_element_type=F32)
    lane = lax.broadcasted_iota(jnp.int32, logits.shape, 1).astype(F32)
    neg = -jnp.inf
    lg = jnp.where(lane < N_EXPERTS, logits, neg)
    m1 = jnp.max(lg, axis=1, keepdims=True)
    i1 = jnp.min(jnp.where(lg == m1, lane, float(LANES)), axis=1, keepdims=True)
    lg2 = jnp.where(lane == i1, neg, lg)
    m2 = jnp.max(lg2, axis=1, keepdims=True)
    i2 = jnp.min(jnp.where(lg2 == m2, lane, float(LANES)), axis=1, keepdims=True)
    e2 = jnp.exp(m2 - m1)
    den = 1.0 + e2
    return jnp.where(lane == i1, 1.0 / den, 0.0) + jnp.where(lane == i2, e2 / den, 0.0)


def _ffn_kernel(*refs, moe):
    if moe:
        (x_ref, g_ref, sh_ref, sc_ref, gt_ref, r_ref, w1_ref, w3_ref, w2_ref,
         xo_ref, h_scr, acc_scr, gate_scr) = refs
    else:
        (x_ref, g_ref, sh_ref, sc_ref, gt_ref, w1_ref, w3_ref, w2_ref,
         xo_ref, h_scr, acc_scr) = refs
    e = pl.program_id(1)
    j = pl.program_id(2)

    @pl.when((e == 0) & (j == 0))
    def _():
        h = _rmsnorm(x_ref[...], g_ref[...]) * (1.0 + sc_ref[...]) + sh_ref[...]
        h_scr[...] = h.astype(BF16)
        acc_scr[...] = jnp.zeros_like(acc_scr)
        if moe:
            gate_scr[...] = _top2_gates(h, r_ref[...])

    h = h_scr[...]
    a = jnp.dot(h, w1_ref[...].astype(BF16), preferred_element_type=F32)
    b = jnp.dot(h, w3_ref[...].astype(BF16), preferred_element_type=F32)
    part = _bdot(_silu(a) * b, w2_ref[...])
    if moe:
        lane = lax.broadcasted_iota(jnp.int32, gate_scr.shape, 1)
        part = part * jnp.sum(jnp.where(lane == e, gate_scr[...], 0.0), axis=1, keepdims=True)
    acc_scr[...] += part

    @pl.when((e == pl.num_programs(1) - 1) & (j == pl.num_programs(2) - 1))
    def _():
        xo_ref[...] = x_ref[...] + gt_ref[...] * acc_scr[...]


def _ffn(x, g2, mod, mod_spec, w1, w3, w2, router=None, *, tm, tf=1408):
    n = x.shape[0]
    n_exp = w1.shape[0]
    moe = router is not None
    in_specs = [
        pl.BlockSpec((tm, D_MODEL), lambda i, e, j: (i, 0)),
        pl.BlockSpec((1, D_MODEL), lambda i, e, j: (0, 0)),
        mod_spec(3), mod_spec(4), mod_spec(5),
    ]
    args = [x, g2, mod, mod, mod]
    scratch = [pltpu.VMEM((tm, D_MODEL), BF16), pltpu.VMEM((tm, D_MODEL), F32)]
    if moe:
        in_specs.append(pl.BlockSpec((D_MODEL, LANES), lambda i, e, j: (0, 0)))
        args.append(router)
        scratch.append(pltpu.VMEM((tm, LANES), F32))
    in_specs += [
        pl.BlockSpec((None, D_MODEL, tf), lambda i, e, j: (e, 0, j)),
        pl.BlockSpec((None, D_MODEL, tf), lambda i, e, j: (e, 0, j)),
        pl.BlockSpec((None, tf, D_MODEL), lambda i, e, j: (e, j, 0)),
    ]
    args += [w1, w3, w2]
    return pl.pallas_call(
        functools.partial(_ffn_kernel, moe=moe),
        grid=(n // tm, n_exp, D_FF // tf),
        in_specs=in_specs,
        out_specs=pl.BlockSpec((tm, D_MODEL), lambda i, e, j: (i, 0)),
        out_shape=jax.ShapeDtypeStruct((n, D_MODEL), F32),
        scratch_shapes=scratch,
        compiler_params=pltpu.CompilerParams(
            dimension_semantics=("arbitrary", "arbitrary", "arbitrary"), vmem_limit_bytes=VMEM_LIMIT),
        name="ffn_moe" if moe else "ffn_dense",
    )(*args)


def _final_norm_kernel(x_ref, g_ref, o_ref):
    o_ref[...] = _rmsnorm(x_ref[...], g_ref[...])


def _final_norm(x, g, *, tm):
    n = x.shape[0]
    return pl.pallas_call(
        _final_norm_kernel,
        grid=(n // tm,),
        in_specs=[pl.BlockSpec((tm, D_MODEL), lambda i: (i, 0)), pl.BlockSpec((1, D_MODEL), lambda i: (0, 0))],
        out_specs=pl.BlockSpec((tm, D_MODEL), lambda i: (i, 0)),
        out_shape=jax.ShapeDtypeStruct((n, D_MODEL), F32),
        compiler_params=pltpu.CompilerParams(
            dimension_semantics=("arbitrary",), vmem_limit_bytes=VMEM_LIMIT),
        name="final_norm",
    )(x, g)


def _block_diag(w):
    nb, bw, _ = w.shape
    eye = jnp.eye(nb, dtype=w.dtype)
    return (eye[:, None, :, None] * w[:, :, None, :]).reshape(nb * bw, nb * bw)


def kernel(x_prompt, x_sample, c_prompt, c_sample, state_rglru_h, state_rglru_conv, state_hgrn,
           mod_w, mod_b, norm1_g, norm2_g, w_in, conv_w, conv_b, rg_wa, rg_ba, rg_wx, rg_bx,
           rg_lambda, rg_out_g, hg_lb_logits, hg_norm_g, w_out, ffn_w1, ffn_w3, ffn_w2,
           router_w, moe_w1, moe_w3, moe_w2, final_g):
    n_b, seq, _ = x_prompt.shape
    n_s = x_sample.shape[0]
    tm_p = 512
    tiles_per_seq = seq // tm_p

    mod = _modulation(jnp.concatenate([c_prompt, c_sample], axis=0), mod_w, mod_b)
    lbs = _lower_bounds(hg_lb_logits)

    def row(v):
        return v.reshape(1, -1)

    xp = x_prompt
    xs = x_sample.reshape(n_s, D_MODEL)
    outs = {k: [] for k in ("hp", "cp", "sp", "hs", "cs", "ss")}
    for l in range(DEPTH):
        mod_p = mod[l, :n_b].reshape(n_b, 1, 6 * D_MODEL)
        mod_s = mod[l, n_b:]
        pvec = jnp.concatenate([
            conv_w[l], row(conv_b[l]), row(rg_ba[l]), row(rg_bx[l]), row(rg_lambda[l]), row(rg_out_g[l]),
            row(lbs[l]), row(hg_norm_g[l]), jnp.zeros((_P_ROWS - 11, W_A), F32)], axis=0)
        wg = jnp.concatenate([_block_diag(rg_wa[l]), _block_diag(rg_wx[l])], axis=1).astype(BF16)
        w_in_l = w_in[l].astype(BF16)
        w_out_l = w_out[l].astype(BF16)

        xp, hp, tail, sp = _mix_prompt(xp, row(norm1_g[l]), mod_p, w_in_l, pvec, wg, w_out_l)
        outs["hp"].append(hp.reshape(n_b, W_A))
        outs["cp"].append(tail[:, SUBLANES - (CONV_W - 1):, :])
        outs["sp"].append(sp)

        u_s = _in_sample(xs, row(norm1_g[l]), mod_s, w_in_l)
        y_s, hs, cs, ss = _mix_sample(u_s, state_rglru_h[l], state_rglru_conv[l].reshape(n_s, -1),
                                      state_hgrn[l], pvec, wg)
        xs = _out_sample(xs, y_s, mod_s, w_out_l)
        outs["hs"].append(hs)
        outs["cs"].append(cs.reshape(n_s, CONV_W - 1, W_A))
        outs["ss"].append(ss)

        if l % 2 == 0:
            w1 = ffn_w1[l // 2].astype(BF16)[None]
            w3 = ffn_w3[l // 2].astype(BF16)[None]
            w2 = ffn_w2[l // 2].astype(BF16)[None]
            router = None
        else:
            w1 = moe_w1[l // 2].astype(BF16)
            w3 = moe_w3[l // 2].astype(BF16)
            w2 = moe_w2[l // 2].astype(BF16)
            router = jnp.pad(router_w[l // 2], ((0, 0), (0, LANES - N_EXPERTS)))
        spec_p = lambda c: pl.BlockSpec((None, 1, D_MODEL), lambda i, e, j: (i // tiles_per_seq, 0, c))
        spec_s = lambda c: pl.BlockSpec((n_s, D_MODEL), lambda i, e, j: (0, c))
        xp = _ffn(xp.reshape(n_b * seq, D_MODEL), row(norm2_g[l]), mod_p, spec_p, w1, w3, w2, router,
                  tm=tm_p).reshape(n_b, seq, D_MODEL)
        xs = _ffn(xs, row(norm2_g[l]), mod_s, spec_s, w1, w3, w2, router, tm=n_s)

    y_prompt = _final_norm(xp.reshape(n_b * seq, D_MODEL), row(final_g), tm=1024).reshape(n_b, seq, D_MODEL)
    y_sample = _final_norm(xs, row(final_g), tm=n_s).reshape(n_s, 1, D_MODEL)
    return (y_prompt, y_sample,
            jnp.stack(outs["hp"]), jnp.stack(outs["cp"]), jnp.stack(outs["sp"]),
            jnp.stack(outs["hs"]), jnp.stack(outs["cs"]), jnp.stack(outs["ss"]))
```

```python
import functools

import jax
import jax.numpy as jnp
from jax import lax
from jax.experimental import pallas as pl
from jax.experimental.pallas import tpu as pltpu

F32 = jnp.float32
BF16 = jnp.bfloat16
HIGHEST = lax.Precision.HIGHEST

D_MODEL = 1024
DEPTH = 4
W_A = 512
RG_BLOCKS = 8
CONV_W = 4
RG_C = 8.0
W_B = 512
HG_HEADS = 4
HG_D = 128
D_IN = 3072
D_FF = 2816
N_EXPERTS = 8
EPS = 1e-6
PAST_LEN = 16384

LANES = 128
SUBLANES = 8
VMEM_LIMIT = 56 * 1024 * 1024

_P_CONV_W, _P_CONV_B, _P_BA, _P_BX, _P_LAM, _P_RG_G, _P_LB, _P_HG_G = 0, 4, 5, 6, 7, 8, 9, 10
_P_ROWS = 16


def _silu(x):
    return x * jax.nn.sigmoid(x)


def _gelu_tanh(x):
    cdf = 0.5 * (1.0 + jnp.tanh(0.7978845608028654 * (x + 0.044715 * (x * x * x))))
    return x * cdf


def _rmsnorm(x, g):
    return x * lax.rsqrt(jnp.mean(x * x, axis=-1, keepdims=True) + EPS) * g


def _softplus(z):
    return jnp.maximum(z, 0.0) + jnp.log1p(jnp.exp(-jnp.abs(z)))


def _bdot(a, b):
    return jnp.dot(a.astype(BF16), b.astype(BF16), preferred_element_type=F32)


def _rglru_coeffs(xc, wg, p_ref):
    gates = _bdot(xc, wg)
    r = jax.nn.sigmoid(gates[:, :W_A] + p_ref[_P_BA:_P_BA + 1, :])
    i = jax.nn.sigmoid(gates[:, W_A:] + p_ref[_P_BX:_P_BX + 1, :])
    log_a = (-RG_C * _softplus(-p_ref[_P_LAM:_P_LAM + 1, :])) * r
    a = jnp.exp(log_a)
    one_minus_a2 = -jnp.tanh(log_a) * (a * a + 1.0)
    return a, one_minus_a2, i * xc


def _head_rmsnorm_gate(o, gb, g_row):
    outs = []
    for h in range(HG_HEADS):
        hs = slice(h * HG_D, (h + 1) * HG_D)
        outs.append(_rmsnorm(o[:, hs], g_row[:, hs]))
    return jnp.concatenate(outs, axis=1) * _silu(gb)


def _mod_kernel(c_ref, w_ref, b_ref, o_ref):
    o_ref[...] = _bdot(_silu(c_ref[...]), w_ref[...]) + b_ref[...]


def _modulation(c_all, mod_w, mod_b):
    rows = c_all.shape[0]
    tn = 1536
    n_out = mod_w.shape[-1]
    return pl.pallas_call(
        _mod_kernel,
        grid=(DEPTH, n_out // tn),
        in_specs=[
            pl.BlockSpec((rows, D_MODEL), lambda l, j: (0, 0)),
            pl.BlockSpec((None, D_MODEL, tn), lambda l, j: (l, 0, j)),
            pl.BlockSpec((None, 1, tn), lambda l, j: (l, 0, j)),
        ],
        out_specs=pl.BlockSpec((None, rows, tn), lambda l, j: (l, 0, j)),
        out_shape=jax.ShapeDtypeStruct((DEPTH, rows, n_out), F32),
        compiler_params=pltpu.CompilerParams(
            dimension_semantics=("arbitrary", "arbitrary"), vmem_limit_bytes=VMEM_LIMIT),
        name="adaln_mod",
    )(c_all, mod_w, mod_b.reshape(DEPTH, 1, n_out))


def _lb_kernel(l_ref, o_ref):
    x = l_ref[...]
    e = jnp.exp(x - jnp.max(x, axis=0, keepdims=True))
    p = e / jnp.sum(e, axis=0, keepdims=True)
    p0 = p[0:1, :]
    run = p0
    o_ref[0:1, :] = run - p0
    for r in range(1, DEPTH):
        run = run + p[r:r + 1, :]
        o_ref[r:r + 1, :] = run - p0


def _lower_bounds(hg_lb_logits):
    return pl.pallas_call(
        _lb_kernel,
        out_shape=jax.ShapeDtypeStruct(hg_lb_logits.shape, F32),
        name="hgrn_lower_bounds",
    )(hg_lb_logits)


def _hgrn_chunk(q, k, v, logf, st):
    C = q.shape[0]
    rc = lax.broadcasted_iota(jnp.int32, (C, C), 0)
    cc = lax.broadcasted_iota(jnp.int32, (C, C), 1)
    tril = (rc >= cc).astype(F32)
    b = jnp.dot(tril, logf, precision=HIGHEST, preferred_element_type=F32)
    row = lax.broadcasted_iota(jnp.int32, (C, W_B), 0)
    xor = rc ^ cc

    heads = [slice(h * HG_D, (h + 1) * HG_D) for h in range(HG_HEADS)]
    nt = (((1,), (1,)), ((), ()))
    qb, kb = q.astype(BF16), k.astype(BF16)
    att = [jnp.where(rc == cc, lax.dot_general(qb[:, hs], kb[:, hs], nt, preferred_element_type=F32), 0.0)
           for hs in heads]

    c = b - logf
    d = b
    m, lg = 1, 0
    while m < C:
        odd = (row & m) != 0
        z = jnp.where(odd, b - c, d - b)
        x = (jnp.where(odd, q, k) * jnp.exp(z)).astype(BF16)
        mask = ((xor >> lg) == 1) & ((rc & m) != 0)
        for h, hs in enumerate(heads):
            p = lax.dot_general(x[:, hs], x[:, hs], nt, preferred_element_type=F32)
            att[h] = jnp.where(mask, p, att[h])
        c = jnp.where(odd, pltpu.roll(c, m, 0), c)
        d = jnp.where(odd, d, pltpu.roll(d, C - m, 0))
        m, lg = 2 * m, lg + 1

    b_last = b[C - 1:C, :]
    qe = (q * jnp.exp(b)).astype(BF16)
    kd = (k * jnp.exp(b_last - b)).astype(BF16)
    vb = v.astype(BF16)
    tn = (((0,), (0,)), ((), ()))
    outs, new_st = [], []
    for h, hs in enumerate(heads):
        o_inter = lax.dot_general(qe[:, hs], st[h].astype(BF16), nt, preferred_element_type=F32)
        o_intra = jnp.dot(att[h].astype(BF16), vb[:, hs], preferred_element_type=F32)
        outs.append(o_inter + o_intra)
        upd = lax.dot_general(vb[:, hs], kd[:, hs], tn, preferred_element_type=F32)
        new_st.append(jnp.exp(b_last[:, hs]) * st[h] + upd)
    return jnp.concatenate(outs, axis=1), new_st


def _shift_rows(x, k, fill):
    row = lax.broadcasted_iota(jnp.int32, x.shape, 0)
    return jnp.where(row >= k, pltpu.roll(x, k, 0), fill)


def _mix_prompt_kernel(x_ref, g1_ref, sh_ref, sc_ref, gt_ref, win_ref, p_ref, wg_ref, wout_ref,
                       xo_ref, hn_ref, tail_ref, sn_ref,
                       hcar, ext, st_scr, *, hg_chunk):
    t = pl.program_id(1)
    tc = x_ref.shape[0]

    @pl.when(t == 0)
    def _():
        hcar[...] = jnp.zeros_like(hcar)
        ext[0:SUBLANES, :] = jnp.zeros((SUBLANES, W_A), F32)
        st_scr[...] = jnp.zeros_like(st_scr)

    x = x_ref[...]
    h = _rmsnorm(x, g1_ref[...]) * (1.0 + sc_ref[...]) + sh_ref[...]
    u = _bdot(h, win_ref[...])

    xa = u[:, 0:W_A]
    ga = u[:, W_A:2 * W_A]
    ext[SUBLANES:, :] = xa
    xc = p_ref[_P_CONV_B:_P_CONV_B + 1, :]
    for j in range(CONV_W):
        lo = SUBLANES - (CONV_W - 1) + j
        xc = xc + ext[lo:lo + tc, :] * p_ref[_P_CONV_W + j:_P_CONV_W + j + 1, :]
    tail = xa[tc - SUBLANES:, :]
    ext[0:SUBLANES, :] = tail
    tail_ref[...] = tail

    a, one_minus_a2, ix = _rglru_coeffs(xc, wg_ref[...], p_ref)
    row = lax.broadcasted_iota(jnp.int32, (tc, W_A), 0)
    first = (row == 0) & (t == 0)
    a = jnp.where(first, 0.0, a)
    bb = jnp.where(first, 1.0, jnp.sqrt(one_minus_a2)) * ix
    k = 1
    while k < tc:
        bb = a * _shift_rows(bb, k, 0.0) + bb
        a = a * _shift_rows(a, k, 1.0)
        k *= 2
    hseq = a * hcar[...] + bb
    h_last = hseq[tc - 1:tc, :]
    hcar[...] = h_last
    hn_ref[...] = h_last
    y_a = _rmsnorm(hseq * _gelu_tanh(ga), p_ref[_P_RG_G:_P_RG_G + 1, :])

    lb = p_ref[_P_LB:_P_LB + 1, :]
    f = lb + (1.0 - lb) * jax.nn.sigmoid(u[:, 3 * W_A:4 * W_A])
    q = _silu(u[:, 2 * W_A:3 * W_A])
    kk = 1.0 - f
    logf = jnp.log(f)
    v = u[:, 4 * W_A:5 * W_A]
    st = [st_scr[hh] for hh in range(HG_HEADS)]
    outs = []
    for c0 in range(0, tc, hg_chunk):
        cs = slice(c0, c0 + hg_chunk)
        o_c, st = _hgrn_chunk(q[cs], kk[cs], v[cs], logf[cs], st)
        outs.append(o_c)
    o = outs[0] if len(outs) == 1 else jnp.concatenate(outs, axis=0)
    for hh in range(HG_HEADS):
        st_scr[hh] = st[hh]
        sn_ref[hh] = st[hh].T
    y_b = _head_rmsnorm_gate(o, u[:, 5 * W_A:6 * W_A], p_ref[_P_HG_G:_P_HG_G + 1, :])

    y = _bdot(jnp.concatenate([y_a, y_b], axis=1), wout_ref[...])
    xo_ref[...] = x + gt_ref[...] * y


def _mix_prompt(x, g1, mod, w_in, pvec, wg, w_out, *, tc=256, hg_chunk=128):
    bsz, seq, _ = x.shape
    const = lambda b, t: (0, 0)
    kern = functools.partial(_mix_prompt_kernel, hg_chunk=hg_chunk)
    return pl.pallas_call(
        kern,
        grid=(bsz, seq // tc),
        in_specs=[
            pl.BlockSpec((None, tc, D_MODEL), lambda b, t: (b, t, 0)),
            pl.BlockSpec((1, D_MODEL), const),
            pl.BlockSpec((None, 1, D_MODEL), lambda b, t: (b, 0, 0)),
            pl.BlockSpec((None, 1, D_MODEL), lambda b, t: (b, 0, 1)),
            pl.BlockSpec((None, 1, D_MODEL), lambda b, t: (b, 0, 2)),
            pl.BlockSpec((D_MODEL, D_IN), const),
            pl.BlockSpec((_P_ROWS, W_A), const),
            pl.BlockSpec((W_A, 2 * W_A), const),
            pl.BlockSpec((D_MODEL, D_MODEL), const),
        ],
        out_specs=[
            pl.BlockSpec((None, tc, D_MODEL), lambda b, t: (b, t, 0)),
            pl.BlockSpec((None, 1, W_A), lambda b, t: (b, 0, 0)),
            pl.BlockSpec((None, SUBLANES, W_A), lambda b, t: (b, 0, 0)),
            pl.BlockSpec((None, HG_HEADS, HG_D, HG_D), lambda b, t: (b, 0, 0, 0)),
        ],
        out_shape=[
            jax.ShapeDtypeStruct(x.shape, F32),
            jax.ShapeDtypeStruct((bsz, 1, W_A), F32),
            jax.ShapeDtypeStruct((bsz, SUBLANES, W_A), F32),
            jax.ShapeDtypeStruct((bsz, HG_HEADS, HG_D, HG_D), F32),
        ],
        scratch_shapes=[
            pltpu.VMEM((1, W_A), F32),
            pltpu.VMEM((tc + SUBLANES, W_A), F32),
            pltpu.VMEM((HG_HEADS, HG_D, HG_D), F32),
        ],
        compiler_params=pltpu.CompilerParams(
            dimension_semantics=("arbitrary", "arbitrary"), vmem_limit_bytes=VMEM_LIMIT),
        name="mix_prompt",
    )(x, g1, mod, mod, mod, w_in, pvec, wg, w_out)


def _in_sample_kernel(x_ref, g1_ref, sh_ref, sc_ref, win_ref, u_ref):
    h = _rmsnorm(x_ref[...], g1_ref[...]) * (1.0 + sc_ref[...]) + sh_ref[...]
    u_ref[...] = _bdot(h, win_ref[...])


def _in_sample(x, g1, mod, w_in):
    n = x.shape[0]
    tn = 1024
    return pl.pallas_call(
        _in_sample_kernel,
        grid=(D_IN // tn,),
        in_specs=[
            pl.BlockSpec((n, D_MODEL), lambda j: (0, 0)),
            pl.BlockSpec((1, D_MODEL), lambda j: (0, 0)),
            pl.BlockSpec((n, D_MODEL), lambda j: (0, 0)),
            pl.BlockSpec((n, D_MODEL), lambda j: (0, 1)),
            pl.BlockSpec((D_MODEL, tn), lambda j: (0, j)),
        ],
        out_specs=pl.BlockSpec((n, tn), lambda j: (0, j)),
        out_shape=jax.ShapeDtypeStruct((n, D_IN), F32),
        compiler_params=pltpu.CompilerParams(
            dimension_semantics=("arbitrary",), vmem_limit_bytes=VMEM_LIMIT),
        name="in_sample",
    )(x, g1, mod, mod, w_in)


def _mix_sample_kernel(u_ref, h0_ref, c0_ref, s0_ref, p_ref, wg_ref,
                       y_ref, hn_ref, cn_ref, sn_ref, o_scr):
    bs = u_ref.shape[0]
    u = u_ref[...]
    xa = u[:, 0:W_A]
    ga = u[:, W_A:2 * W_A]
    c0 = c0_ref[...]
    xc = p_ref[_P_CONV_B:_P_CONV_B + 1, :]
    for j in range(CONV_W - 1):
        xc = xc + c0[:, j * W_A:(j + 1) * W_A] * p_ref[_P_CONV_W + j:_P_CONV_W + j + 1, :]
    xc = xc + xa * p_ref[_P_CONV_W + CONV_W - 1:_P_CONV_W + CONV_W, :]
    cn_ref[:, 0:2 * W_A] = c0[:, W_A:]
    cn_ref[:, 2 * W_A:] = xa

    a, one_minus_a2, ix = _rglru_coeffs(xc, wg_ref[...], p_ref)
    hnew = jnp.sqrt(one_minus_a2) * ix + a * h0_ref[...]
    hn_ref[...] = hnew
    y_a = _rmsnorm(hnew * _gelu_tanh(ga), p_ref[_P_RG_G:_P_RG_G + 1, :])

    lb = p_ref[_P_LB:_P_LB + 1, :]
    f = lb + (1.0 - lb) * jax.nn.sigmoid(u[:, 3 * W_A:4 * W_A])
    q = _silu(u[:, 2 * W_A:3 * W_A])
    kk = 1.0 - f
    v = u[:, 4 * W_A:5 * W_A]
    eye = (lax.broadcasted_iota(jnp.int32, (bs, bs), 0) == lax.broadcasted_iota(jnp.int32, (bs, bs), 1)).astype(F32)
    cols = lax.dot_general(jnp.concatenate([f, kk, q], axis=1), eye, (((0,), (0,)), ((), ())),
                           precision=HIGHEST, preferred_element_type=F32)
    for j in range(bs):
        for hh in range(HG_HEADS):
            hs = slice(hh * HG_D, (hh + 1) * HG_D)
            fcol = cols[hh * HG_D:(hh + 1) * HG_D, j:j + 1]
            kcol = cols[W_B + hh * HG_D:W_B + (hh + 1) * HG_D, j:j + 1]
            qcol = cols[2 * W_B + hh * HG_D:2 * W_B + (hh + 1) * HG_D, j:j + 1]
            s_new = fcol * s0_ref[j, hh] + kcol * v[j:j + 1, hs]
            sn_ref[j, hh] = s_new
            o_scr[j:j + 1, hs] = jnp.sum(qcol * s_new, axis=0, keepdims=True)
    y_b = _head_rmsnorm_gate(o_scr[...], u[:, 5 * W_A:6 * W_A], p_ref[_P_HG_G:_P_HG_G + 1, :])
    y_ref[:, 0:W_A] = y_a
    y_ref[:, W_A:] = y_b


def _mix_sample(u, h0, c0, s0, pvec, wg, *, bs=8):
    n = u.shape[0]
    const = lambda i: (0, 0)
    return pl.pallas_call(
        _mix_sample_kernel,
        grid=(n // bs,),
        in_specs=[
            pl.BlockSpec((bs, D_IN), lambda i: (i, 0)),
            pl.BlockSpec((bs, W_A), lambda i: (i, 0)),
            pl.BlockSpec((bs, (CONV_W - 1) * W_A), lambda i: (i, 0)),
            pl.BlockSpec((bs, HG_HEADS, HG_D, HG_D), lambda i: (i, 0, 0, 0)),
            pl.BlockSpec((_P_ROWS, W_A), const),
            pl.BlockSpec((W_A, 2 * W_A), const),
        ],
        out_specs=[
            pl.BlockSpec((bs, D_MODEL), lambda i: (i, 0)),
            pl.BlockSpec((bs, W_A), lambda i: (i, 0)),
            pl.BlockSpec((bs, (CONV_W - 1) * W_A), lambda i: (i, 0)),
            pl.BlockSpec((bs, HG_HEADS, HG_D, HG_D), lambda i: (i, 0, 0, 0)),
        ],
        out_shape=[
            jax.ShapeDtypeStruct((n, D_MODEL), F32),
            jax.ShapeDtypeStruct((n, W_A), F32),
            jax.ShapeDtypeStruct((n, (CONV_W - 1) * W_A), F32),
            jax.ShapeDtypeStruct((n, HG_HEADS, HG_D, HG_D), F32),
        ],
        scratch_shapes=[pltpu.VMEM((bs, W_B), F32)],
        compiler_params=pltpu.CompilerParams(
            dimension_semantics=("arbitrary",), vmem_limit_bytes=VMEM_LIMIT),
        name="mix_sample",
    )(u, h0, c0, s0, pvec, wg)


def _out_sample_kernel(x_ref, y_ref, gt_ref, wout_ref, xo_ref):
    xo_ref[...] = x_ref[...] + gt_ref[...] * _bdot(y_ref[...], wout_ref[...])


def _out_sample(x, y, mod, w_out):
    n = x.shape[0]
    return pl.pallas_call(
        _out_sample_kernel,
        grid=(1,),
        in_specs=[
            pl.BlockSpec((n, D_MODEL), lambda i: (0, 0)),
            pl.BlockSpec((n, D_MODEL), lambda i: (0, 0)),
            pl.BlockSpec((n, D_MODEL), lambda i: (0, 2)),
            pl.BlockSpec((D_MODEL, D_MODEL), lambda i: (0, 0)),
        ],
        out_specs=pl.BlockSpec((n, D_MODEL), lambda i: (0, 0)),
        out_shape=jax.ShapeDtypeStruct((n, D_MODEL), F32),
        compiler_params=pltpu.CompilerParams(
            dimension_semantics=("arbitrary",), vmem_limit_bytes=VMEM_LIMIT),
        name="out_sample",
    )(x, y, mod, w_out)


def _top2(h, router):
    logits = jnp.dot(h, router, precision=HIGHEST, preferred_element_type=F32)
    lane = lax.broadcasted_iota(jnp.int32, logits.shape, 1).astype(F32)
    neg = -jnp.inf
    lg = jnp.where(lane < N_EXPERTS, logits, neg)
    m1 = jnp.max(lg, axis=1, keepdims=True)
    i1 = jnp.min(jnp.where(lg == m1, lane, float(LANES)), axis=1, keepdims=True)
    lg2 = jnp.where(lane == i1, neg, lg)
    m2 = jnp.max(lg2, axis=1, keepdims=True)
    i2 = jnp.min(jnp.where(lg2 == m2, lane, float(LANES)), axis=1, keepdims=True)
    e2 = jnp.exp(m2 - m1)
    den = 1.0 + e2
    return lane, i1, i2, 1.0 / den, e2 / den


def _top2_gates(h, router):
    lane, i1, i2, w1, w2 = _top2(h, router)
    return jnp.where(lane == i1, w1, 0.0) + jnp.where(lane == i2, w2, 0.0)


def _ffn_kernel(*refs, moe):
    if moe:
        (x_ref, g_ref, sh_ref, sc_ref, gt_ref, r_ref, w1_ref, w3_ref, w2_ref,
         xo_ref, h_scr, acc_scr, gate_scr) = refs
    else:
        (x_ref, g_ref, sh_ref, sc_ref, gt_ref, w1_ref, w3_ref, w2_ref,
         xo_ref, h_scr, acc_scr) = refs
    e = pl.program_id(1)
    j = pl.program_id(2)

    @pl.when((e == 0) & (j == 0))
    def _():
        h = _rmsnorm(x_ref[...], g_ref[...]) * (1.0 + sc_ref[...]) + sh_ref[...]
        h_scr[...] = h.astype(BF16)
        acc_scr[...] = jnp.zeros_like(acc_scr)
        if moe:
            gate_scr[...] = _top2_gates(h, r_ref[...])

    h = h_scr[...]
    a = jnp.dot(h, w1_ref[...].astype(BF16), preferred_element_type=F32)
    b = jnp.dot(h, w3_ref[...].astype(BF16), preferred_element_type=F32)
    part = _bdot(_silu(a) * b, w2_ref[...])
    if moe:
        lane = lax.broadcasted_iota(jnp.int32, gate_scr.shape, 1)
        part = part * jnp.sum(jnp.where(lane == e, gate_scr[...], 0.0), axis=1, keepdims=True)
    acc_scr[...] += part

    @pl.when((e == pl.num_programs(1) - 1) & (j == pl.num_programs(2) - 1))
    def _():
        xo_ref[...] = x_ref[...] + gt_ref[...] * acc_scr[...]


def _ffn(x, g2, mod, mod_spec, w1, w3, w2, router=None, *, tm, tf=1408):
    n = x.shape[0]
    n_exp = w1.shape[0]
    moe = router is not None
    in_specs = [
        pl.BlockSpec((tm, D_MODEL), lambda i, e, j: (i, 0)),
        pl.BlockSpec((1, D_MODEL), lambda i, e, j: (0, 0)),
        mod_spec(3), mod_spec(4), mod_spec(5),
    ]
    args = [x, g2, mod, mod, mod]
    scratch = [pltpu.VMEM((tm, D_MODEL), BF16), pltpu.VMEM((tm, D_MODEL), F32)]
    if moe:
        in_specs.append(pl.BlockSpec((D_MODEL, LANES), lambda i, e, j: (0, 0)))
        args.append(router)
        scratch.append(pltpu.VMEM((tm, LANES), F32))
    in_specs += [
        pl.BlockSpec((None, D_MODEL, tf), lambda i, e, j: (e, 0, j)),
        pl.BlockSpec((None, D_MODEL, tf), lambda i, e, j: (e, 0, j)),
        pl.BlockSpec((None, tf, D_MODEL), lambda i, e, j: (e, j, 0)),
    ]
    args += [w1, w3, w2]
    return pl.pallas_call(
        functools.partial(_ffn_kernel, moe=moe),
        grid=(n // tm, n_exp, D_FF // tf),
        in_specs=in_specs,
        out_specs=pl.BlockSpec((tm, D_MODEL), lambda i, e, j: (i, 0)),
        out_shape=jax.ShapeDtypeStruct((n, D_MODEL), F32),
        scratch_shapes=scratch,
        compiler_params=pltpu.CompilerParams(
            dimension_semantics=("arbitrary", "arbitrary", "arbitrary"), vmem_limit_bytes=VMEM_LIMIT),
        name="ffn_moe" if moe else "ffn_dense",
    )(*args)


_R_I1, _R_I2, _R_W1, _R_W2, _R_RANK1, _R_RANK2 = 0, 1, 2, 3, 4, 5


def _route_kernel(x_ref, g_ref, sh_ref, sc_ref, r_ref, h_ref, route_ref, cnt_ref, run_scr):
    i = pl.program_id(0)
    tm = x_ref.shape[0]

    @pl.when(i == 0)
    def _():
        run_scr[...] = jnp.zeros_like(run_scr)

    h = _rmsnorm(x_ref[...], g_ref[...]) * (1.0 + sc_ref[...]) + sh_ref[...]
    h_ref[...] = h
    lane, i_hi, i_lo, w_hi, w_lo = _top2(h, r_ref[...])
    cnt = ((lane == i_hi) | (lane == i_lo)).astype(BF16)
    rr = lax.broadcasted_iota(jnp.int32, (tm, tm), 0)
    cc = lax.broadcasted_iota(jnp.int32, (tm, tm), 1)
    before = jnp.dot((rr > cc).astype(BF16), cnt, preferred_element_type=F32) + run_scr[...]
    rank_hi = jnp.sum(jnp.where(lane == i_hi, before, 0.0), axis=1, keepdims=True)
    rank_lo = jnp.sum(jnp.where(lane == i_lo, before, 0.0), axis=1, keepdims=True)
    run_scr[...] += jnp.sum(cnt.astype(F32), axis=0, keepdims=True)
    cnt_ref[...] = run_scr[...]
    rec = jnp.zeros(lane.shape, F32)
    for ln, val in ((_R_I1, i_hi), (_R_I2, i_lo), (_R_W1, w_hi), (_R_W2, w_lo),
                    (_R_RANK1, rank_hi), (_R_RANK2, rank_lo)):
        rec = jnp.where(lane == float(ln), val, rec)
    route_ref[...] = rec


def _route(x, g2, mod, mod_spec, router, *, tm):
    n = x.shape[0]
    return pl.pallas_call(
        _route_kernel,
        grid=(n // tm,),
        in_specs=[
            pl.BlockSpec((tm, D_MODEL), lambda i: (i, 0)),
            pl.BlockSpec((1, D_MODEL), lambda i: (0, 0)),
            mod_spec(3), mod_spec(4),
            pl.BlockSpec((D_MODEL, LANES), lambda i: (0, 0)),
        ],
        out_specs=[
            pl.BlockSpec((tm, D_MODEL), lambda i: (i, 0)),
            pl.BlockSpec((tm, LANES), lambda i: (i, 0)),
            pl.BlockSpec((1, LANES), lambda i: (0, 0)),
        ],
        out_shape=[
            jax.ShapeDtypeStruct((n, D_MODEL), F32),
            jax.ShapeDtypeStruct((n, LANES), F32),
            jax.ShapeDtypeStruct((1, LANES), F32),
        ],
        scratch_shapes=[pltpu.VMEM((1, LANES), F32)],
        compiler_params=pltpu.CompilerParams(
            dimension_semantics=("arbitrary",), vmem_limit_bytes=VMEM_LIMIT),
        name="moe_route",
    )(x, g2, mod, mod, router)


def _dispatch_kernel(pos_ref, h_ref, xs_in_ref, xs_ref, sem):
    del xs_in_ref
    rows = h_ref.shape[0]

    def row_copy(r, k):
        return pltpu.make_async_copy(
            h_ref.at[pl.ds(r, 1), :], xs_ref.at[pl.ds(pos_ref[2 * r + k], 1), :], sem)

    def start(r, carry):
        row_copy(r, 0).start()
        row_copy(r, 1).start()
        return carry

    def wait(r, carry):
        row_copy(r, 0).wait()
        row_copy(r, 1).wait()
        return carry

    lax.fori_loop(0, rows, start, 0)
    lax.fori_loop(0, rows, wait, 0)


def _dispatch(h, pos, n_rows, *, tm):
    n = h.shape[0]
    xs0 = jnp.zeros((n_rows, D_MODEL), F32)
    return pl.pallas_call(
        _dispatch_kernel,
        grid=(n // tm,),
        in_specs=[
            pl.BlockSpec((2 * tm,), lambda i: (i,), memory_space=pltpu.SMEM),
            pl.BlockSpec((tm, D_MODEL), lambda i: (i, 0)),
            pl.BlockSpec(memory_space=pl.ANY),
        ],
        out_specs=pl.BlockSpec(memory_space=pl.ANY),
        out_shape=jax.ShapeDtypeStruct((n_rows, D_MODEL), F32),
        scratch_shapes=[pltpu.SemaphoreType.DMA(())],
        input_output_aliases={2: 0},
        compiler_params=pltpu.CompilerParams(
            dimension_semantics=("arbitrary",), vmem_limit_bytes=VMEM_LIMIT),
        name="moe_dispatch",
    )(pos, h, xs0)


def _gffn_kernel(te_ref, nv_ref, xs_ref, w1_ref, w3_ref, w2_ref, ys_ref, h_scr, acc_scr):
    del te_ref
    i = pl.program_id(0)
    j = pl.program_id(1)

    @pl.when(i < nv_ref[0])
    def _():
        @pl.when(j == 0)
        def _():
            h_scr[...] = xs_ref[...].astype(BF16)
            acc_scr[...] = jnp.zeros_like(acc_scr)

        h = h_scr[...]
        a = jnp.dot(h, w1_ref[...].astype(BF16), preferred_element_type=F32)
        b = jnp.dot(h, w3_ref[...].astype(BF16), preferred_element_type=F32)
        acc_scr[...] += _bdot(_silu(a) * b, w2_ref[...])

        @pl.when(j == pl.num_programs(1) - 1)
        def _():
            ys_ref[...] = acc_scr[...]

    @pl.when((i >= nv_ref[0]) & (j == 0))
    def _():
        ys_ref[...] = jnp.zeros_like(ys_ref)


def _gffn(xs, tile_expert, n_valid, w1, w3, w2, *, tm, tf=1408):
    n_rows = xs.shape[0]

    def row_idx(i, j, te, nv):
        return (jnp.minimum(i, nv[0] - 1), 0)

    grid_spec = pltpu.PrefetchScalarGridSpec(
        num_scalar_prefetch=2,
        grid=(n_rows // tm, D_FF // tf),
        in_specs=[
            pl.BlockSpec((tm, D_MODEL), row_idx),
            pl.BlockSpec((None, D_MODEL, tf), lambda i, j, te, nv: (te[i], 0, jnp.where(i < nv[0], j, D_FF // tf - 1))),
            pl.BlockSpec((None, D_MODEL, tf), lambda i, j, te, nv: (te[i], 0, jnp.where(i < nv[0], j, D_FF // tf - 1))),
            pl.BlockSpec((None, tf, D_MODEL), lambda i, j, te, nv: (te[i], jnp.where(i < nv[0], j, D_FF // tf - 1), 0)),
        ],
        out_specs=pl.BlockSpec((tm, D_MODEL), lambda i, j, te, nv: (i, 0)),
        scratch_shapes=[pltpu.VMEM((tm, D_MODEL), BF16), pltpu.VMEM((tm, D_MODEL), F32)],
    )
    return pl.pallas_call(
        _gffn_kernel,
        grid_spec=grid_spec,
        out_shape=jax.ShapeDtypeStruct((n_rows, D_MODEL), F32),
        compiler_params=pltpu.CompilerParams(
            dimension_semantics=("arbitrary", "arbitrary"), vmem_limit_bytes=VMEM_LIMIT),
        name="moe_grouped_ffn",
    )(tile_expert, n_valid, xs, w1, w3, w2)


def _combine_kernel(pos_ref, x_ref, gt_ref, route_ref, ys_ref, xo_ref, y0_scr, y1_scr, sem):
    rows = x_ref.shape[0]

    def row_copy(r, k):
        buf = y0_scr if k == 0 else y1_scr
        return pltpu.make_async_copy(
            ys_ref.at[pl.ds(pos_ref[2 * r + k], 1), :], buf.at[pl.ds(r, 1), :], sem)

    def start(r, carry):
        row_copy(r, 0).start()
        row_copy(r, 1).start()
        return carry

    def wait(r, carry):
        row_copy(r, 0).wait()
        row_copy(r, 1).wait()
        return carry

    lax.fori_loop(0, rows, start, 0)
    lax.fori_loop(0, rows, wait, 0)
    route = route_ref[...]
    fo = route[:, _R_W1:_R_W1 + 1] * y0_scr[...] + route[:, _R_W2:_R_W2 + 1] * y1_scr[...]
    xo_ref[...] = x_ref[...] + gt_ref[...] * fo


def _combine(x, mod, mod_spec, route, pos, ys, *, tm):
    n = x.shape[0]
    return pl.pallas_call(
        _combine_kernel,
        grid=(n // tm,),
        in_specs=[
            pl.BlockSpec((2 * tm,), lambda i: (i,), memory_space=pltpu.SMEM),
            pl.BlockSpec((tm, D_MODEL), lambda i: (i, 0)),
            mod_spec(5),
            pl.BlockSpec((tm, LANES), lambda i: (i, 0)),
            pl.BlockSpec(memory_space=pl.ANY),
        ],
        out_specs=pl.BlockSpec((tm, D_MODEL), lambda i: (i, 0)),
        out_shape=jax.ShapeDtypeStruct((n, D_MODEL), F32),
        scratch_shapes=[pltpu.VMEM((tm, D_MODEL), F32), pltpu.VMEM((tm, D_MODEL), F32),
                        pltpu.SemaphoreType.DMA(())],
        compiler_params=pltpu.CompilerParams(
            dimension_semantics=("arbitrary",), vmem_limit_bytes=VMEM_LIMIT),
        name="moe_combine",
    )(pos, x, mod, route, ys)


def _moe_routed(x, g2, mod, tiles_per_seq_of, router, w1, w3, w2, *, tm_route=512, tm_move=256, tm_group=512):
    n = x.shape[0]

    def spec1(tm):
        return lambda c: pl.BlockSpec((None, 1, D_MODEL), lambda i: (i // tiles_per_seq_of(tm), 0, c))

    h, route, counts = _route(x, g2, mod, spec1(tm_route), router, tm=tm_route)
    cnt = counts[0, :N_EXPERTS].astype(jnp.int32)
    tiles_e = (cnt + tm_group - 1) // tm_group
    tile_start = jnp.cumsum(tiles_e) - tiles_e
    n_valid = jnp.sum(tiles_e)
    n_tiles = (2 * n) // tm_group + N_EXPERTS
    tile_ids = jnp.minimum(jnp.arange(n_tiles, dtype=jnp.int32), n_valid - 1)
    tile_expert = (jnp.sum(tile_ids[:, None] >= tile_start[None, :], axis=1) - 1).astype(jnp.int32)
    idx = route[:, _R_I1:_R_I2 + 1].astype(jnp.int32)
    rank = route[:, _R_RANK1:_R_RANK2 + 1].astype(jnp.int32)
    pos = ((tile_start * tm_group)[idx] + rank).reshape(2 * n)

    xs = _dispatch(h, pos, n_tiles * tm_group, tm=tm_move)
    ys = _gffn(xs, tile_expert, n_valid.reshape(1).astype(jnp.int32), w1, w3, w2, tm=tm_group)
    return _combine(x, mod, spec1(tm_move), route, pos, ys, tm=tm_move)


def _final_norm_kernel(x_ref, g_ref, o_ref):
    o_ref[...] = _rmsnorm(x_ref[...], g_ref[...])


def _final_norm(x, g, *, tm):
    n = x.shape[0]
    return pl.pallas_call(
        _final_norm_kernel,
        grid=(n // tm,),
        in_specs=[pl.BlockSpec((tm, D_MODEL), lambda i: (i, 0)), pl.BlockSpec((1, D_MODEL), lambda i: (0, 0))],
        out_specs=pl.BlockSpec((tm, D_MODEL), lambda i: (i, 0)),
        out_shape=jax.ShapeDtypeStruct((n, D_MODEL), F32),
        compiler_params=pltpu.CompilerParams(
            dimension_semantics=("arbitrary",), vmem_limit_bytes=VMEM_LIMIT),
        name="final_norm",
    )(x, g)


def _block_diag(w):
    nb, bw, _ = w.shape
    eye = jnp.eye(nb, dtype=w.dtype)
    return (eye[:, None, :, None] * w[:, :, None, :]).reshape(nb * bw, nb * bw)


def kernel(x_prompt, x_sample, c_prompt, c_sample, state_rglru_h, state_rglru_conv, state_hgrn,
           mod_w, mod_b, norm1_g, norm2_g, w_in, conv_w, conv_b, rg_wa, rg_ba, rg_wx, rg_bx,
           rg_lambda, rg_out_g, hg_lb_logits, hg_norm_g, w_out, ffn_w1, ffn_w3, ffn_w2,
           router_w, moe_w1, moe_w3, moe_w2, final_g):
    n_b, seq, _ = x_prompt.shape
    n_s = x_sample.shape[0]
    tm_p = 512
    tiles_per_seq = seq // tm_p

    mod = _modulation(jnp.concatenate([c_prompt, c_sample], axis=0), mod_w, mod_b)
    lbs = _lower_bounds(hg_lb_logits)

    def row(v):
        return v.reshape(1, -1)

    xp = x_prompt
    xs = x_sample.reshape(n_s, D_MODEL)
    outs = {k: [] for k in ("hp", "cp", "sp", "hs", "cs", "ss")}
    for l in range(DEPTH):
        mod_p = mod[l, :n_b].reshape(n_b, 1, 6 * D_MODEL)
        mod_s = mod[l, n_b:]
        pvec = jnp.concatenate([
            conv_w[l], row(conv_b[l]), row(rg_ba[l]), row(rg_bx[l]), row(rg_lambda[l]), row(rg_out_g[l]),
            row(lbs[l]), row(hg_norm_g[l]), jnp.zeros((_P_ROWS - 11, W_A), F32)], axis=0)
        wg = jnp.concatenate([_block_diag(rg_wa[l]), _block_diag(rg_wx[l])], axis=1).astype(BF16)
        w_in_l = w_in[l].astype(BF16)
        w_out_l = w_out[l].astype(BF16)

        xp, hp, tail, sp = _mix_prompt(xp, row(norm1_g[l]), mod_p, w_in_l, pvec, wg, w_out_l)
        outs["hp"].append(hp.reshape(n_b, W_A))
        outs["cp"].append(tail[:, SUBLANES - (CONV_W - 1):, :])
        outs["sp"].append(sp)

        u_s = _in_sample(xs, row(norm1_g[l]), mod_s, w_in_l)
        y_s, hs, cs, ss = _mix_sample(u_s, state_rglru_h[l], state_rglru_conv[l].reshape(n_s, -1),
                                      state_hgrn[l], pvec, wg)
        xs = _out_sample(xs, y_s, mod_s, w_out_l)
        outs["hs"].append(hs)
        outs["cs"].append(cs.reshape(n_s, CONV_W - 1, W_A))
        outs["ss"].append(ss)

        if l % 2 == 0:
            w1 = ffn_w1[l // 2].astype(BF16)[None]
            w3 = ffn_w3[l // 2].astype(BF16)[None]
            w2 = ffn_w2[l // 2].astype(BF16)[None]
            router = None
        else:
            w1 = moe_w1[l // 2].astype(BF16)
            w3 = moe_w3[l // 2].astype(BF16)
            w2 = moe_w2[l // 2].astype(BF16)
            router = jnp.pad(router_w[l // 2], ((0, 0), (0, LANES - N_EXPERTS)))
        spec_p = lambda c: pl.BlockSpec((None, 1, D_MODEL), lambda i, e, j: (i // tiles_per_seq, 0, c))
        spec_s = lambda c: pl.BlockSpec((n_s, D_MODEL), lambda i, e, j: (0, c))
        if router is None:
            xp = _ffn(xp.reshape(n_b * seq, D_MODEL), row(norm2_g[l]), mod_p, spec_p, w1, w3, w2, None, tm=tm_p)
        else:
            xp = _moe_routed(xp.reshape(n_b * seq, D_MODEL), row(norm2_g[l]), mod_p, lambda tm: seq // tm,
                             router, w1, w3, w2)
        xp = xp.reshape(n_b, seq, D_MODEL)
        xs = _ffn(xs, row(norm2_g[l]), mod_s, spec_s, w1, w3, w2, router, tm=n_s)

    y_prompt = _final_norm(xp.reshape(n_b * seq, D_MODEL), row(final_g), tm=1024).reshape(n_b, seq, D_MODEL)
    y_sample = _final_norm(xs, row(final_g), tm=n_s).reshape(n_s, 1, D_MODEL)
    return (y_prompt, y_sample,
            jnp.stack(outs["hp"]), jnp.stack(outs["cp"]), jnp.stack(outs["sp"]),
            jnp.stack(outs["hs"]), jnp.stack(outs["cs"]), jnp.stack(outs["ss"]))
```

```python
import functools

import jax
import jax.numpy as jnp
from jax import lax
from jax.experimental import pallas as pl
from jax.experimental.pallas import tpu as pltpu

F32 = jnp.float32
BF16 = jnp.bfloat16
HIGHEST = lax.Precision.HIGHEST

D_MODEL = 1024
DEPTH = 4
W_A = 512
RG_BLOCKS = 8
CONV_W = 4
RG_C = 8.0
W_B = 512
HG_HEADS = 4
HG_D = 128
D_IN = 3072
D_FF = 2816
N_EXPERTS = 8
EPS = 1e-6
PAST_LEN = 16384

LANES = 128
SUBLANES = 8
VMEM_LIMIT = 56 * 1024 * 1024

_P_CONV_W, _P_CONV_B, _P_BA, _P_BX, _P_LAM, _P_RG_G, _P_LB, _P_HG_G = 0, 4, 5, 6, 7, 8, 9, 10
_P_ROWS = 16


def _silu(x):
    return x * jax.nn.sigmoid(x)


def _gelu_tanh(x):
    cdf = 0.5 * (1.0 + jnp.tanh(0.7978845608028654 * (x + 0.044715 * (x * x * x))))
    return x * cdf


def _rmsnorm(x, g):
    return x * lax.rsqrt(jnp.mean(x * x, axis=-1, keepdims=True) + EPS) * g


def _softplus(z):
    return jnp.maximum(z, 0.0) + jnp.log1p(jnp.exp(-jnp.abs(z)))


def _bdot(a, b):
    return jnp.dot(a.astype(BF16), b.astype(BF16), preferred_element_type=F32)


def _rglru_coeffs(xc, wg, p_ref):
    gates = _bdot(xc, wg)
    r = jax.nn.sigmoid(gates[:, :W_A] + p_ref[_P_BA:_P_BA + 1, :])
    i = jax.nn.sigmoid(gates[:, W_A:] + p_ref[_P_BX:_P_BX + 1, :])
    log_a = (-RG_C * _softplus(-p_ref[_P_LAM:_P_LAM + 1, :])) * r
    a = jnp.exp(log_a)
    one_minus_a2 = -jnp.tanh(log_a) * (a * a + 1.0)
    return a, one_minus_a2, i * xc


def _sqrt_nonneg(x):
    return jnp.where(x == 0.0, 0.0, x * lax.rsqrt(x))


def _group_roll(x, k):
    r, w = x.shape
    return pltpu.roll(x.reshape(r // SUBLANES, SUBLANES, w), k, 1).reshape(r, w)


def _scan_rows(a, bb, h0):
    r, w = a.shape
    sub = lax.broadcasted_iota(jnp.int32, (r, w), 0) & (SUBLANES - 1)
    k = 1
    while k < SUBLANES:
        keep = sub >= k
        bb = a * jnp.where(keep, _group_roll(bb, k), 0.0) + bb
        a = a * jnp.where(keep, _group_roll(a, k), 1.0)
        k *= 2
    outs = []
    carry = h0
    for g in range(r // SUBLANES):
        rows = slice(g * SUBLANES, (g + 1) * SUBLANES)
        hg = a[rows] * carry + bb[rows]
        outs.append(hg)
        carry = hg[SUBLANES - 1:SUBLANES, :]
    return jnp.concatenate(outs, axis=0)


def _head_rmsnorm_gate(o, gb, g_row):
    outs = []
    for h in range(HG_HEADS):
        hs = slice(h * HG_D, (h + 1) * HG_D)
        outs.append(_rmsnorm(o[:, hs], g_row[:, hs]))
    return jnp.concatenate(outs, axis=1) * _silu(gb)


def _mod_kernel(c_ref, w_ref, b_ref, o_ref):
    o_ref[...] = _bdot(_silu(c_ref[...]), w_ref[...]) + b_ref[...]


def _modulation(c_all, mod_w, mod_b):
    rows = c_all.shape[0]
    tn = 1536
    n_out = mod_w.shape[-1]
    return pl.pallas_call(
        _mod_kernel,
        grid=(DEPTH, n_out // tn),
        in_specs=[
            pl.BlockSpec((rows, D_MODEL), lambda l, j: (0, 0)),
            pl.BlockSpec((None, D_MODEL, tn), lambda l, j: (l, 0, j)),
            pl.BlockSpec((None, 1, tn), lambda l, j: (l, 0, j)),
        ],
        out_specs=pl.BlockSpec((None, rows, tn), lambda l, j: (l, 0, j)),
        out_shape=jax.ShapeDtypeStruct((DEPTH, rows, n_out), F32),
        compiler_params=pltpu.CompilerParams(
            dimension_semantics=("arbitrary", "arbitrary"), vmem_limit_bytes=VMEM_LIMIT),
        name="adaln_mod",
    )(c_all, mod_w, mod_b.reshape(DEPTH, 1, n_out))


def _lb_kernel(l_ref, o_ref):
    x = l_ref[...]
    e = jnp.exp(x - jnp.max(x, axis=0, keepdims=True))
    p = e / jnp.sum(e, axis=0, keepdims=True)
    p0 = p[0:1, :]
    run = p0
    o_ref[0:1, :] = run - p0
    for r in range(1, DEPTH):
        run = run + p[r:r + 1, :]
        o_ref[r:r + 1, :] = run - p0


def _lower_bounds(hg_lb_logits):
    return pl.pallas_call(
        _lb_kernel,
        out_shape=jax.ShapeDtypeStruct(hg_lb_logits.shape, F32),
        name="hgrn_lower_bounds",
    )(hg_lb_logits)


def _hgrn_chunk(q, k, v, logf, st):
    C = q.shape[0]
    G = C // SUBLANES
    rc = lax.broadcasted_iota(jnp.int32, (C, C), 0)
    cc = lax.broadcasted_iota(jnp.int32, (C, C), 1)
    tril = (rc >= cc).astype(F32)
    b = jnp.dot(tril, logf, precision=HIGHEST, preferred_element_type=F32)
    gr = lax.broadcasted_iota(jnp.int32, (2 * G, C), 0)
    gc = lax.broadcasted_iota(jnp.int32, (2 * G, C), 1)
    bound = jnp.where(gr < G, gr * SUBLANES, (gr - G + 1) * SUBLANES)
    gb = jnp.dot((gc < bound).astype(F32), logf, precision=HIGHEST, preferred_element_type=F32)
    g_start, g_end = gb[:G], gb[G:]
    row = lax.broadcasted_iota(jnp.int32, (C, W_B), 0)
    xor = rc ^ cc

    heads = [slice(h * HG_D, (h + 1) * HG_D) for h in range(HG_HEADS)]
    nt = (((1,), (1,)), ((), ()))
    qb, kb = q.astype(BF16), k.astype(BF16)
    att = [jnp.where(rc == cc, lax.dot_general(qb[:, hs], kb[:, hs], nt, preferred_element_type=F32), 0.0)
           for hs in heads]

    def add_level(x, m, lg):
        mask = ((xor >> lg) == 1) & ((rc & m) != 0)
        for h, hs in enumerate(heads):
            p = lax.dot_general(x[:, hs], x[:, hs], nt, preferred_element_type=F32)
            att[h] = jnp.where(mask, p, att[h])

    c = b - logf
    d = b
    m, lg = 1, 0
    while m < SUBLANES:
        odd = (row & m) != 0
        z = jnp.where(odd, b - c, d - b)
        add_level((jnp.where(odd, q, k) * jnp.exp(z)).astype(BF16), m, lg)
        c = jnp.where(odd, _group_roll(c, m), c)
        d = jnp.where(odd, d, _group_roll(d, SUBLANES - m))
        m, lg = 2 * m, lg + 1

    q8 = q * jnp.exp(b - c)
    k8 = k * jnp.exp(d - b)
    grow = lax.broadcasted_iota(jnp.int32, (G, W_B), 0)
    cg, dg = g_start, g_end
    mu = 1
    while mu < G:
        oddg = (grow & mu) != 0
        rg = jnp.exp(jnp.where(oddg, g_start - cg, dg - g_end))
        pieces = []
        for g in range(G):
            base = q8 if (g // mu) % 2 == 1 else k8
            pieces.append(base[g * SUBLANES:(g + 1) * SUBLANES] * rg[g:g + 1, :])
        add_level(jnp.concatenate(pieces, axis=0).astype(BF16), m, lg)
        cg = jnp.where(oddg, pltpu.roll(cg, mu, 0), cg)
        dg = jnp.where(oddg, dg, pltpu.roll(dg, G - mu, 0))
        mu, m, lg = 2 * mu, 2 * m, lg + 1

    b_last = b[C - 1:C, :]
    qe = (q * jnp.exp(b)).astype(BF16)
    kd = (k * jnp.exp(b_last - b)).astype(BF16)
    vb = v.astype(BF16)
    tn = (((0,), (0,)), ((), ()))
    outs, new_st = [], []
    for h, hs in enumerate(heads):
        o_inter = lax.dot_general(qe[:, hs], st[h].astype(BF16), nt, preferred_element_type=F32)
        o_intra = jnp.dot(att[h].astype(BF16), vb[:, hs], preferred_element_type=F32)
        outs.append(o_inter + o_intra)
        upd = lax.dot_general(vb[:, hs], kd[:, hs], tn, preferred_element_type=F32)
        new_st.append(jnp.exp(b_last[:, hs]) * st[h] + upd)
    return jnp.concatenate(outs, axis=1), new_st


def _mix_prompt_kernel(x_ref, g1_ref, sh_ref, sc_ref, gt_ref, win_ref, p_ref, wg_ref, wout_ref,
                       xo_ref, hn_ref, tail_ref, sn_ref,
                       hcar, ext, st_scr, *, hg_chunk):
    t = pl.program_id(1)
    tc = x_ref.shape[0]

    @pl.when(t == 0)
    def _():
        hcar[...] = jnp.zeros_like(hcar)
        ext[0:SUBLANES, :] = jnp.zeros((SUBLANES, W_A), F32)
        st_scr[...] = jnp.zeros_like(st_scr)

    x = x_ref[...]
    h = _rmsnorm(x, g1_ref[...]) * (1.0 + sc_ref[...]) + sh_ref[...]
    u = _bdot(h, win_ref[...])

    xa = u[:, 0:W_A]
    ga = u[:, W_A:2 * W_A]
    ext[SUBLANES:, :] = xa
    xc = p_ref[_P_CONV_B:_P_CONV_B + 1, :]
    for j in range(CONV_W):
        lo = SUBLANES - (CONV_W - 1) + j
        xc = xc + ext[lo:lo + tc, :] * p_ref[_P_CONV_W + j:_P_CONV_W + j + 1, :]
    tail = xa[tc - SUBLANES:, :]
    ext[0:SUBLANES, :] = tail
    tail_ref[...] = tail

    a, one_minus_a2, ix = _rglru_coeffs(xc, wg_ref[...], p_ref)
    row = lax.broadcasted_iota(jnp.int32, (tc, W_A), 0)
    first = (row == 0) & (t == 0)
    a = jnp.where(first, 0.0, a)
    bb = jnp.where(first, 1.0, _sqrt_nonneg(one_minus_a2)) * ix
    hseq = _scan_rows(a, bb, hcar[...])
    h_last = hseq[tc - 1:tc, :]
    hcar[...] = h_last
    hn_ref[...] = h_last
    y_a = _rmsnorm(hseq * _gelu_tanh(ga), p_ref[_P_RG_G:_P_RG_G + 1, :])

    lb = p_ref[_P_LB:_P_LB + 1, :]
    f = lb + (1.0 - lb) * jax.nn.sigmoid(u[:, 3 * W_A:4 * W_A])
    q = _silu(u[:, 2 * W_A:3 * W_A])
    kk = 1.0 - f
    logf = jnp.log(f)
    v = u[:, 4 * W_A:5 * W_A]
    st = [st_scr[hh] for hh in range(HG_HEADS)]
    outs = []
    for c0 in range(0, tc, hg_chunk):
        cs = slice(c0, c0 + hg_chunk)
        o_c, st = _hgrn_chunk(q[cs], kk[cs], v[cs], logf[cs], st)
        outs.append(o_c)
    o = outs[0] if len(outs) == 1 else jnp.concatenate(outs, axis=0)
    for hh in range(HG_HEADS):
        st_scr[hh] = st[hh]
        sn_ref[hh] = st[hh].T
    y_b = _head_rmsnorm_gate(o, u[:, 5 * W_A:6 * W_A], p_ref[_P_HG_G:_P_HG_G + 1, :])

    y = _bdot(jnp.concatenate([y_a, y_b], axis=1), wout_ref[...])
    xo_ref[...] = x + gt_ref[...] * y


def _mix_prompt(x, g1, mod, w_in, pvec, wg, w_out, layer, *, tc=256, hg_chunk=128):
    bsz, seq, _ = x.shape
    const = lambda b, t: (0, 0)
    wconst = lambda b, t: (layer, 0, 0)
    kern = functools.partial(_mix_prompt_kernel, hg_chunk=hg_chunk)
    return pl.pallas_call(
        kern,
        grid=(bsz, seq // tc),
        in_specs=[
            pl.BlockSpec((None, tc, D_MODEL), lambda b, t: (b, t, 0)),
            pl.BlockSpec((1, D_MODEL), const),
            pl.BlockSpec((None, 1, D_MODEL), lambda b, t: (b, 0, 0)),
            pl.BlockSpec((None, 1, D_MODEL), lambda b, t: (b, 0, 1)),
            pl.BlockSpec((None, 1, D_MODEL), lambda b, t: (b, 0, 2)),
            pl.BlockSpec((None, D_MODEL, D_IN), wconst),
            pl.BlockSpec((_P_ROWS, W_A), const),
            pl.BlockSpec((W_A, 2 * W_A), const),
            pl.BlockSpec((None, D_MODEL, D_MODEL), wconst),
        ],
        out_specs=[
            pl.BlockSpec((None, tc, D_MODEL), lambda b, t: (b, t, 0)),
            pl.BlockSpec((None, 1, W_A), lambda b, t: (b, 0, 0)),
            pl.BlockSpec((None, SUBLANES, W_A), lambda b, t: (b, 0, 0)),
            pl.BlockSpec((None, HG_HEADS, HG_D, HG_D), lambda b, t: (b, 0, 0, 0)),
        ],
        out_shape=[
            jax.ShapeDtypeStruct(x.shape, F32),
            jax.ShapeDtypeStruct((bsz, 1, W_A), F32),
            jax.ShapeDtypeStruct((bsz, SUBLANES, W_A), F32),
            jax.ShapeDtypeStruct((bsz, HG_HEADS, HG_D, HG_D), F32),
        ],
        scratch_shapes=[
            pltpu.VMEM((1, W_A), F32),
            pltpu.VMEM((tc + SUBLANES, W_A), F32),
            pltpu.VMEM((HG_HEADS, HG_D, HG_D), F32),
        ],
        compiler_params=pltpu.CompilerParams(
            dimension_semantics=("arbitrary", "arbitrary"), vmem_limit_bytes=VMEM_LIMIT),
        name="mix_prompt",
    )(x, g1, mod, mod, mod, w_in, pvec, wg, w_out)


def _in_sample_kernel(x_ref, g1_ref, sh_ref, sc_ref, win_ref, u_ref):
    h = _rmsnorm(x_ref[...], g1_ref[...]) * (1.0 + sc_ref[...]) + sh_ref[...]
    u_ref[...] = _bdot(h, win_ref[...])


def _in_sample(x, g1, mod, w_in, layer):
    n = x.shape[0]
    tn = 1024
    return pl.pallas_call(
        _in_sample_kernel,
        grid=(D_IN // tn,),
        in_specs=[
            pl.BlockSpec((n, D_MODEL), lambda j: (0, 0)),
            pl.BlockSpec((1, D_MODEL), lambda j: (0, 0)),
            pl.BlockSpec((n, D_MODEL), lambda j: (0, 0)),
            pl.BlockSpec((n, D_MODEL), lambda j: (0, 1)),
            pl.BlockSpec((None, D_MODEL, tn), lambda j: (layer, 0, j)),
        ],
        out_specs=pl.BlockSpec((n, tn), lambda j: (0, j)),
        out_shape=jax.ShapeDtypeStruct((n, D_IN), F32),
        compiler_params=pltpu.CompilerParams(
            dimension_semantics=("arbitrary",), vmem_limit_bytes=VMEM_LIMIT),
        name="in_sample",
    )(x, g1, mod, mod, w_in)


def _mix_sample_kernel(u_ref, h0_ref, c0_ref, s0_ref, p_ref, wg_ref,
                       y_ref, hn_ref, cn_ref, sn_ref, o_scr):
    bs = u_ref.shape[0]
    u = u_ref[...]
    xa = u[:, 0:W_A]
    ga = u[:, W_A:2 * W_A]
    c0 = c0_ref[...]
    xc = p_ref[_P_CONV_B:_P_CONV_B + 1, :]
    for j in range(CONV_W - 1):
        xc = xc + c0[:, j * W_A:(j + 1) * W_A] * p_ref[_P_CONV_W + j:_P_CONV_W + j + 1, :]
    xc = xc + xa * p_ref[_P_CONV_W + CONV_W - 1:_P_CONV_W + CONV_W, :]
    cn_ref[:, 0:2 * W_A] = c0[:, W_A:]
    cn_ref[:, 2 * W_A:] = xa

    a, one_minus_a2, ix = _rglru_coeffs(xc, wg_ref[...], p_ref)
    hnew = _sqrt_nonneg(one_minus_a2) * ix + a * h0_ref[...]
    hn_ref[...] = hnew
    y_a = _rmsnorm(hnew * _gelu_tanh(ga), p_ref[_P_RG_G:_P_RG_G + 1, :])

    lb = p_ref[_P_LB:_P_LB + 1, :]
    f = lb + (1.0 - lb) * jax.nn.sigmoid(u[:, 3 * W_A:4 * W_A])
    q = _silu(u[:, 2 * W_A:3 * W_A])
    kk = 1.0 - f
    v = u[:, 4 * W_A:5 * W_A]
    eye = (lax.broadcasted_iota(jnp.int32, (bs, bs), 0) == lax.broadcasted_iota(jnp.int32, (bs, bs), 1)).astype(F32)
    cols = lax.dot_general(jnp.concatenate([f, kk, q], axis=1), eye, (((0,), (0,)), ((), ())),
                           precision=HIGHEST, preferred_element_type=F32)
    for j in range(bs):
        for hh in range(HG_HEADS):
            hs = slice(hh * HG_D, (hh + 1) * HG_D)
            fcol = cols[hh * HG_D:(hh + 1) * HG_D, j:j + 1]
            kcol = cols[W_B + hh * HG_D:W_B + (hh + 1) * HG_D, j:j + 1]
            qcol = cols[2 * W_B + hh * HG_D:2 * W_B + (hh + 1) * HG_D, j:j + 1]
            s_new = fcol * s0_ref[j, hh] + kcol * v[j:j + 1, hs]
            sn_ref[j, hh] = s_new
            o_scr[j:j + 1, hs] = jnp.sum(qcol * s_new, axis=0, keepdims=True)
    y_b = _head_rmsnorm_gate(o_scr[...], u[:, 5 * W_A:6 * W_A], p_ref[_P_HG_G:_P_HG_G + 1, :])
    y_ref[:, 0:W_A] = y_a
    y_ref[:, W_A:] = y_b


def _mix_sample(u, h0, c0, s0, pvec, wg, layer, *, bs=8):
    n = u.shape[0]
    const = lambda i: (0, 0)
    return pl.pallas_call(
        _mix_sample_kernel,
        grid=(n // bs,),
        in_specs=[
            pl.BlockSpec((bs, D_IN), lambda i: (i, 0)),
            pl.BlockSpec((None, bs, W_A), lambda i: (layer, i, 0)),
            pl.BlockSpec((None, bs, (CONV_W - 1) * W_A), lambda i: (layer, i, 0)),
            pl.BlockSpec((None, bs, HG_HEADS, HG_D, HG_D), lambda i: (layer, i, 0, 0, 0)),
            pl.BlockSpec((_P_ROWS, W_A), const),
            pl.BlockSpec((W_A, 2 * W_A), const),
        ],
        out_specs=[
            pl.BlockSpec((bs, D_MODEL), lambda i: (i, 0)),
            pl.BlockSpec((bs, W_A), lambda i: (i, 0)),
            pl.BlockSpec((bs, (CONV_W - 1) * W_A), lambda i: (i, 0)),
            pl.BlockSpec((bs, HG_HEADS, HG_D, HG_D), lambda i: (i, 0, 0, 0)),
        ],
        out_shape=[
            jax.ShapeDtypeStruct((n, D_MODEL), F32),
            jax.ShapeDtypeStruct((n, W_A), F32),
            jax.ShapeDtypeStruct((n, (CONV_W - 1) * W_A), F32),
            jax.ShapeDtypeStruct((n, HG_HEADS, HG_D, HG_D), F32),
        ],
        scratch_shapes=[pltpu.VMEM((bs, W_B), F32)],
        compiler_params=pltpu.CompilerParams(
            dimension_semantics=("arbitrary",), vmem_limit_bytes=VMEM_LIMIT),
        name="mix_sample",
    )(u, h0, c0, s0, pvec, wg)


def _out_sample_kernel(x_ref, y_ref, gt_ref, wout_ref, xo_ref):
    xo_ref[...] = x_ref[...] + gt_ref[...] * _bdot(y_ref[...], wout_ref[...])


def _out_sample(x, y, mod, w_out, layer):
    n = x.shape[0]
    return pl.pallas_call(
        _out_sample_kernel,
        grid=(1,),
        in_specs=[
            pl.BlockSpec((n, D_MODEL), lambda i: (0, 0)),
            pl.BlockSpec((n, D_MODEL), lambda i: (0, 0)),
            pl.BlockSpec((n, D_MODEL), lambda i: (0, 2)),
            pl.BlockSpec((None, D_MODEL, D_MODEL), lambda i: (layer, 0, 0)),
        ],
        out_specs=pl.BlockSpec((n, D_MODEL), lambda i: (0, 0)),
        out_shape=jax.ShapeDtypeStruct((n, D_MODEL), F32),
        compiler_params=pltpu.CompilerParams(
            dimension_semantics=("arbitrary",), vmem_limit_bytes=VMEM_LIMIT),
        name="out_sample",
    )(x, y, mod, w_out)


def _top2(h, router):
    logits = jnp.dot(h, router, precision=HIGHEST, preferred_element_type=F32)
    lane = lax.broadcasted_iota(jnp.int32, logits.shape, 1).astype(F32)
    neg = -jnp.inf
    lg = jnp.where(lane < N_EXPERTS, logits, neg)
    m1 = jnp.max(lg, axis=1, keepdims=True)
    i1 = jnp.min(jnp.where(lg == m1, lane, float(LANES)), axis=1, keepdims=True)
    lg2 = jnp.where(lane == i1, neg, lg)
    m2 = jnp.max(lg2, axis=1, keepdims=True)
    i2 = jnp.min(jnp.where(lg2 == m2, lane, float(LANES)), axis=1, keepdims=True)
    e2 = jnp.exp(m2 - m1)
    den = 1.0 + e2
    return lane, i1, i2, 1.0 / den, e2 / den


def _top2_gates(h, router):
    lane, i1, i2, w1, w2 = _top2(h, router)
    return jnp.where(lane == i1, w1, 0.0) + jnp.where(lane == i2, w2, 0.0)


def _ffn_kernel(*refs, moe):
    if moe:
        (x_ref, g_ref, sh_ref, sc_ref, gt_ref, r_ref, w1_ref, w3_ref, w2_ref,
         xo_ref, h_scr, acc_scr, gate_scr) = refs
    else:
        (x_ref, g_ref, sh_ref, sc_ref, gt_ref, w1_ref, w3_ref, w2_ref,
         xo_ref, h_scr, acc_scr) = refs
    e = pl.program_id(1)
    j = pl.program_id(2)

    @pl.when((e == 0) & (j == 0))
    def _():
        h = _rmsnorm(x_ref[...], g_ref[...]) * (1.0 + sc_ref[...]) + sh_ref[...]
        h_scr[...] = h.astype(BF16)
        acc_scr[...] = jnp.zeros_like(acc_scr)
        if moe:
            gate_scr[...] = _top2_gates(h, r_ref[...])

    h = h_scr[...]
    a = jnp.dot(h, w1_ref[...].astype(BF16), preferred_element_type=F32)
    b = jnp.dot(h, w3_ref[...].astype(BF16), preferred_element_type=F32)
    part = _bdot(_silu(a) * b, w2_ref[...])
    if moe:
        lane = lax.broadcasted_iota(jnp.int32, gate_scr.shape, 1)
        part = part * jnp.sum(jnp.where(lane == e, gate_scr[...], 0.0), axis=1, keepdims=True)
    acc_scr[...] += part

    @pl.when((e == pl.num_programs(1) - 1) & (j == pl.num_programs(2) - 1))
    def _():
        xo_ref[...] = x_ref[...] + gt_ref[...] * acc_scr[...]


def _ffn(x, g2, mod, mod_spec, w1, w3, w2, wl, router=None, *, tm, tf=1408):
    n = x.shape[0]
    n_exp = w1.shape[1]
    moe = router is not None
    in_specs = [
        pl.BlockSpec((tm, D_MODEL), lambda i, e, j: (i, 0)),
        pl.BlockSpec((1, D_MODEL), lambda i, e, j: (0, 0)),
        mod_spec(3), mod_spec(4), mod_spec(5),
    ]
    args = [x, g2, mod, mod, mod]
    scratch = [pltpu.VMEM((tm, D_MODEL), BF16), pltpu.VMEM((tm, D_MODEL), F32)]
    if moe:
        in_specs.append(pl.BlockSpec((D_MODEL, LANES), lambda i, e, j: (0, 0)))
        args.append(router)
        scratch.append(pltpu.VMEM((tm, LANES), F32))
    in_specs += [
        pl.BlockSpec((None, None, D_MODEL, tf), lambda i, e, j: (wl, e, 0, j)),
        pl.BlockSpec((None, None, D_MODEL, tf), lambda i, e, j: (wl, e, 0, j)),
        pl.BlockSpec((None, None, tf, D_MODEL), lambda i, e, j: (wl, e, j, 0)),
    ]
    args += [w1, w3, w2]
    return pl.pallas_call(
        functools.partial(_ffn_kernel, moe=moe),
        grid=(n // tm, n_exp, D_FF // tf),
        in_specs=in_specs,
        out_specs=pl.BlockSpec((tm, D_MODEL), lambda i, e, j: (i, 0)),
        out_shape=jax.ShapeDtypeStruct((n, D_MODEL), F32),
        scratch_shapes=scratch,
        compiler_params=pltpu.CompilerParams(
            dimension_semantics=("arbitrary", "arbitrary", "arbitrary"), vmem_limit_bytes=VMEM_LIMIT),
        name="ffn_moe" if moe else "ffn_dense",
    )(*args)


_R_I1, _R_I2, _R_W1, _R_W2, _R_RANK1, _R_RANK2 = 0, 1, 2, 3, 4, 5
_ROW_DMA_UNROLL = 8


def _route_kernel(x_ref, g_ref, sh_ref, sc_ref, r_ref, h_ref, route_ref, cnt_ref, run_scr):
    i = pl.program_id(0)
    tm = x_ref.shape[0]

    @pl.when(i == 0)
    def _():
        run_scr[...] = jnp.zeros_like(run_scr)

    h = _rmsnorm(x_ref[...], g_ref[...]) * (1.0 + sc_ref[...]) + sh_ref[...]
    h_ref[...] = h
    lane, i_hi, i_lo, w_hi, w_lo = _top2(h, r_ref[...])
    cnt = ((lane == i_hi) | (lane == i_lo)).astype(BF16)
    rr = lax.broadcasted_iota(jnp.int32, (tm, tm), 0)
    cc = lax.broadcasted_iota(jnp.int32, (tm, tm), 1)
    before = jnp.dot((rr > cc).astype(BF16), cnt, preferred_element_type=F32) + run_scr[...]
    rank_hi = jnp.sum(jnp.where(lane == i_hi, before, 0.0), axis=1, keepdims=True)
    rank_lo = jnp.sum(jnp.where(lane == i_lo, before, 0.0), axis=1, keepdims=True)
    run_scr[...] += jnp.sum(cnt.astype(F32), axis=0, keepdims=True)
    cnt_ref[...] = run_scr[...]
    rec = jnp.zeros(lane.shape, F32)
    for ln, val in ((_R_I1, i_hi), (_R_I2, i_lo), (_R_W1, w_hi), (_R_W2, w_lo),
                    (_R_RANK1, rank_hi), (_R_RANK2, rank_lo)):
        rec = jnp.where(lane == float(ln), val, rec)
    route_ref[...] = rec


def _route(x, g2, mod, mod_spec, router, *, tm):
    n = x.shape[0]
    return pl.pallas_call(
        _route_kernel,
        grid=(n // tm,),
        in_specs=[
            pl.BlockSpec((tm, D_MODEL), lambda i: (i, 0)),
            pl.BlockSpec((1, D_MODEL), lambda i: (0, 0)),
            mod_spec(3), mod_spec(4),
            pl.BlockSpec((D_MODEL, LANES), lambda i: (0, 0)),
        ],
        out_specs=[
            pl.BlockSpec((tm, D_MODEL), lambda i: (i, 0)),
            pl.BlockSpec((tm, LANES), lambda i: (i, 0)),
            pl.BlockSpec((1, LANES), lambda i: (0, 0)),
        ],
        out_shape=[
            jax.ShapeDtypeStruct((n, D_MODEL), F32),
            jax.ShapeDtypeStruct((n, LANES), F32),
            jax.ShapeDtypeStruct((1, LANES), F32),
        ],
        scratch_shapes=[pltpu.VMEM((1, LANES), F32)],
        compiler_params=pltpu.CompilerParams(
            dimension_semantics=("arbitrary",), vmem_limit_bytes=VMEM_LIMIT),
        name="moe_route",
    )(x, g2, mod, mod, router)


def _dispatch_kernel(pos_ref, h_ref, xs_in_ref, xs_ref, sem):
    del xs_in_ref
    rows = h_ref.shape[0]

    def row_copy(r, k):
        return pltpu.make_async_copy(
            h_ref.at[pl.ds(r, 1), :], xs_ref.at[pl.ds(pos_ref[2 * r + k], 1), :], sem)

    def start(r, carry):
        row_copy(r, 0).start()
        row_copy(r, 1).start()
        return carry

    def wait(r, carry):
        row_copy(r, 0).wait()
        row_copy(r, 1).wait()
        return carry

    lax.fori_loop(0, rows, start, 0, unroll=_ROW_DMA_UNROLL)
    lax.fori_loop(0, rows, wait, 0, unroll=_ROW_DMA_UNROLL)


def _dispatch(h, pos, n_rows, *, tm):
    n = h.shape[0]
    xs0 = jnp.zeros((n_rows, D_MODEL), F32)
    return pl.pallas_call(
        _dispatch_kernel,
        grid=(n // tm,),
        in_specs=[
            pl.BlockSpec((2 * tm,), lambda i: (i,), memory_space=pltpu.SMEM),
            pl.BlockSpec((tm, D_MODEL), lambda i: (i, 0)),
            pl.BlockSpec(memory_space=pl.ANY),
        ],
        out_specs=pl.BlockSpec(memory_space=pl.ANY),
        out_shape=jax.ShapeDtypeStruct((n_rows, D_MODEL), F32),
        scratch_shapes=[pltpu.SemaphoreType.DMA(())],
        input_output_aliases={2: 0},
        compiler_params=pltpu.CompilerParams(
            dimension_semantics=("arbitrary",), vmem_limit_bytes=VMEM_LIMIT),
        name="moe_dispatch",
    )(pos, h, xs0)


def _gffn_kernel(te_ref, nv_ref, xs_ref, w1_ref, w3_ref, w2_ref, ys_ref, h_scr, acc_scr):
    del te_ref
    i = pl.program_id(0)
    j = pl.program_id(1)

    @pl.when(i < nv_ref[0])
    def _():
        @pl.when(j == 0)
        def _():
            h_scr[...] = xs_ref[...].astype(BF16)
            acc_scr[...] = jnp.zeros_like(acc_scr)

        h = h_scr[...]
        a = jnp.dot(h, w1_ref[...].astype(BF16), preferred_element_type=F32)
        b = jnp.dot(h, w3_ref[...].astype(BF16), preferred_element_type=F32)
        acc_scr[...] += _bdot(_silu(a) * b, w2_ref[...])

        @pl.when(j == pl.num_programs(1) - 1)
        def _():
            ys_ref[...] = acc_scr[...]

    @pl.when((i >= nv_ref[0]) & (j == 0))
    def _():
        ys_ref[...] = jnp.zeros_like(ys_ref)


def _gffn(xs, tile_expert, n_valid, w1, w3, w2, wl, *, tm, tf=1408):
    n_rows = xs.shape[0]
    last_j = D_FF // tf - 1

    def row_idx(i, j, te, nv):
        return (jnp.minimum(i, nv[0] - 1), 0)

    def ff_idx(i, j, nv):
        return jnp.where(i < nv[0], j, last_j)

    grid_spec = pltpu.PrefetchScalarGridSpec(
        num_scalar_prefetch=2,
        grid=(n_rows // tm, D_FF // tf),
        in_specs=[
            pl.BlockSpec((tm, D_MODEL), row_idx),
            pl.BlockSpec((None, None, D_MODEL, tf), lambda i, j, te, nv: (wl, te[i], 0, ff_idx(i, j, nv))),
            pl.BlockSpec((None, None, D_MODEL, tf), lambda i, j, te, nv: (wl, te[i], 0, ff_idx(i, j, nv))),
            pl.BlockSpec((None, None, tf, D_MODEL), lambda i, j, te, nv: (wl, te[i], ff_idx(i, j, nv), 0)),
        ],
        out_specs=pl.BlockSpec((tm, D_MODEL), lambda i, j, te, nv: (i, 0)),
        scratch_shapes=[pltpu.VMEM((tm, D_MODEL), BF16), pltpu.VMEM((tm, D_MODEL), F32)],
    )
    return pl.pallas_call(
        _gffn_kernel,
        grid_spec=grid_spec,
        out_shape=jax.ShapeDtypeStruct((n_rows, D_MODEL), F32),
        compiler_params=pltpu.CompilerParams(
            dimension_semantics=("arbitrary", "arbitrary"), vmem_limit_bytes=VMEM_LIMIT),
        name="moe_grouped_ffn",
    )(tile_expert, n_valid, xs, w1, w3, w2)


def _combine_kernel(pos_ref, x_ref, gt_ref, route_ref, ys_ref, xo_ref, y0_scr, y1_scr, sem):
    rows = x_ref.shape[0]

    def row_copy(r, k):
        buf = y0_scr if k == 0 else y1_scr
        return pltpu.make_async_copy(
            ys_ref.at[pl.ds(pos_ref[2 * r + k], 1), :], buf.at[pl.ds(r, 1), :], sem)

    def start(r, carry):
        row_copy(r, 0).start()
        row_copy(r, 1).start()
        return carry

    def wait(r, carry):
        row_copy(r, 0).wait()
        row_copy(r, 1).wait()
        return carry

    lax.fori_loop(0, rows, start, 0, unroll=_ROW_DMA_UNROLL)
    lax.fori_loop(0, rows, wait, 0, unroll=_ROW_DMA_UNROLL)
    route = route_ref[...]
    fo = route[:, _R_W1:_R_W1 + 1] * y0_scr[...] + route[:, _R_W2:_R_W2 + 1] * y1_scr[...]
    xo_ref[...] = x_ref[...] + gt_ref[...] * fo


def _combine(x, mod, mod_spec, route, pos, ys, *, tm):
    n = x.shape[0]
    return pl.pallas_call(
        _combine_kernel,
        grid=(n // tm,),
        in_specs=[
            pl.BlockSpec((2 * tm,), lambda i: (i,), memory_space=pltpu.SMEM),
            pl.BlockSpec((tm, D_MODEL), lambda i: (i, 0)),
            mod_spec(5),
            pl.BlockSpec((tm, LANES), lambda i: (i, 0)),
            pl.BlockSpec(memory_space=pl.ANY),
        ],
        out_specs=pl.BlockSpec((tm, D_MODEL), lambda i: (i, 0)),
        out_shape=jax.ShapeDtypeStruct((n, D_MODEL), F32),
        scratch_shapes=[pltpu.VMEM((tm, D_MODEL), F32), pltpu.VMEM((tm, D_MODEL), F32),
                        pltpu.SemaphoreType.DMA(())],
        compiler_params=pltpu.CompilerParams(
            dimension_semantics=("arbitrary",), vmem_limit_bytes=VMEM_LIMIT),
        name="moe_combine",
    )(pos, x, mod, route, ys)


def _moe_routed(x, g2, mod, tiles_per_seq_of, router, w1, w3, w2, wl, *, tm_route=512, tm_move=256, tm_group=512):
    n = x.shape[0]

    def spec1(tm):
        return lambda c: pl.BlockSpec((None, 1, D_MODEL), lambda i: (i // tiles_per_seq_of(tm), 0, c))

    h, route, counts = _route(x, g2, mod, spec1(tm_route), router, tm=tm_route)
    cnt = counts[0, :N_EXPERTS].astype(jnp.int32)
    tiles_e = (cnt + tm_group - 1) // tm_group
    tile_start = jnp.cumsum(tiles_e) - tiles_e
    n_valid = jnp.sum(tiles_e)
    n_tiles = (2 * n) // tm_group + N_EXPERTS
    tile_ids = jnp.minimum(jnp.arange(n_tiles, dtype=jnp.int32), n_valid - 1)
    tile_expert = (jnp.sum(tile_ids[:, None] >= tile_start[None, :], axis=1) - 1).astype(jnp.int32)
    idx = route[:, _R_I1:_R_I2 + 1].astype(jnp.int32)
    rank = route[:, _R_RANK1:_R_RANK2 + 1].astype(jnp.int32)
    pos = ((tile_start * tm_group)[idx] + rank).reshape(2 * n)

    xs = _dispatch(h, pos, n_tiles * tm_group, tm=tm_move)
    ys = _gffn(xs, tile_expert, n_valid.reshape(1).astype(jnp.int32), w1, w3, w2, wl, tm=tm_group)
    return _combine(x, mod, spec1(tm_move), route, pos, ys, tm=tm_move)


def _final_norm_kernel(x_ref, g_ref, o_ref):
    o_ref[...] = _rmsnorm(x_ref[...], g_ref[...])


def _final_norm(x, g, *, tm):
    n = x.shape[0]
    return pl.pallas_call(
        _final_norm_kernel,
        grid=(n // tm,),
        in_specs=[pl.BlockSpec((tm, D_MODEL), lambda i: (i, 0)), pl.BlockSpec((1, D_MODEL), lambda i: (0, 0))],
        out_specs=pl.BlockSpec((tm, D_MODEL), lambda i: (i, 0)),
        out_shape=jax.ShapeDtypeStruct((n, D_MODEL), F32),
        compiler_params=pltpu.CompilerParams(
            dimension_semantics=("arbitrary",), vmem_limit_bytes=VMEM_LIMIT),
        name="final_norm",
    )(x, g)


def _block_diag(w):
    nb, bw, _ = w.shape
    eye = jnp.eye(nb, dtype=w.dtype)
    return (eye[:, None, :, None] * w[:, :, None, :]).reshape(nb * bw, nb * bw)


def kernel(x_prompt, x_sample, c_prompt, c_sample, state_rglru_h, state_rglru_conv, state_hgrn,
           mod_w, mod_b, norm1_g, norm2_g, w_in, conv_w, conv_b, rg_wa, rg_ba, rg_wx, rg_bx,
           rg_lambda, rg_out_g, hg_lb_logits, hg_norm_g, w_out, ffn_w1, ffn_w3, ffn_w2,
           router_w, moe_w1, moe_w3, moe_w2, final_g):
    n_b, seq, _ = x_prompt.shape
    n_s = x_sample.shape[0]
    tm_p = 512
    tiles_per_seq = seq // tm_p

    mod = _modulation(jnp.concatenate([c_prompt, c_sample], axis=0), mod_w, mod_b)
    lbs = _lower_bounds(hg_lb_logits)

    def row(v):
        return v.reshape(1, -1)

    w_in_b = w_in.astype(BF16)
    w_out_b = w_out.astype(BF16)
    dense_w = [w.astype(BF16)[:, None] for w in (ffn_w1, ffn_w3, ffn_w2)]
    moe_w = [w.astype(BF16) for w in (moe_w1, moe_w3, moe_w2)]
    conv_all = state_rglru_conv.reshape(DEPTH, n_s, (CONV_W - 1) * W_A)

    xp = x_prompt
    xs = x_sample.reshape(n_s, D_MODEL)
    outs = {k: [] for k in ("hp", "cp", "sp", "hs", "cs", "ss")}
    for l in range(DEPTH):
        mod_p = mod[l, :n_b].reshape(n_b, 1, 6 * D_MODEL)
        mod_s = mod[l, n_b:]
        pvec = jnp.concatenate([
            conv_w[l], row(conv_b[l]), row(rg_ba[l]), row(rg_bx[l]), row(rg_lambda[l]), row(rg_out_g[l]),
            row(lbs[l]), row(hg_norm_g[l]), jnp.zeros((_P_ROWS - 11, W_A), F32)], axis=0)
        wg = jnp.concatenate([_block_diag(rg_wa[l]), _block_diag(rg_wx[l])], axis=1).astype(BF16)

        xp, hp, tail, sp = _mix_prompt(xp, row(norm1_g[l]), mod_p, w_in_b, pvec, wg, w_out_b, l)
        outs["hp"].append(hp.reshape(n_b, W_A))
        outs["cp"].append(tail[:, SUBLANES - (CONV_W - 1):, :])
        outs["sp"].append(sp)

        u_s = _in_sample(xs, row(norm1_g[l]), mod_s, w_in_b, l)
        y_s, hs, cs, ss = _mix_sample(u_s, state_rglru_h, conv_all, state_hgrn, pvec, wg, l)
        xs = _out_sample(xs, y_s, mod_s, w_out_b, l)
        outs["hs"].append(hs)
        outs["cs"].append(cs.reshape(n_s, CONV_W - 1, W_A))
        outs["ss"].append(ss)

        if l % 2 == 0:
            w1, w3, w2 = dense_w
            router = None
        else:
            w1, w3, w2 = moe_w
            router = jnp.pad(router_w[l // 2], ((0, 0), (0, LANES - N_EXPERTS)))
        spec_p = lambda c: pl.BlockSpec((None, 1, D_MODEL), lambda i, e, j: (i // tiles_per_seq, 0, c))
        spec_s = lambda c: pl.BlockSpec((n_s, D_MODEL), lambda i, e, j: (0, c))
        if router is None:
            xp = _ffn(xp.reshape(n_b * seq, D_MODEL), row(norm2_g[l]), mod_p, spec_p, w1, w3, w2, l // 2, None,
                      tm=tm_p)
        else:
            xp = _moe_routed(xp.reshape(n_b * seq, D_MODEL), row(norm2_g[l]), mod_p, lambda tm: seq // tm,
                             router, w1, w3, w2, l // 2)
        xp = xp.reshape(n_b, seq, D_MODEL)
        xs = _ffn(xs, row(norm2_g[l]), mod_s, spec_s, w1, w3, w2, l // 2, router, tm=n_s)

    y_prompt = _final_norm(xp.reshape(n_b * seq, D_MODEL), row(final_g), tm=1024).reshape(n_b, seq, D_MODEL)
    y_sample = _final_norm(xs, row(final_g), tm=n_s).reshape(n_s, 1, D_MODEL)
    return (y_prompt, y_sample,
            jnp.stack(outs["hp"]), jnp.stack(outs["cp"]), jnp.stack(outs["sp"]),
            jnp.stack(outs["hs"]), jnp.stack(outs["cs"]), jnp.stack(outs["ss"]))
```

```python
import functools

import jax
import jax.numpy as jnp
from jax import lax
from jax.experimental import pallas as pl
from jax.experimental.pallas import tpu as pltpu

F32 = jnp.float32
BF16 = jnp.bfloat16
HIGHEST = lax.Precision.HIGHEST

D_MODEL = 1024
DEPTH = 4
W_A = 512
RG_BLOCKS = 8
CONV_W = 4
RG_C = 8.0
W_B = 512
HG_HEADS = 4
HG_D = 128
D_IN = 3072
D_FF = 2816
N_EXPERTS = 8
EPS = 1e-6
PAST_LEN = 16384

LANES = 128
SUBLANES = 8
VMEM_LIMIT = 56 * 1024 * 1024

_P_CONV_W, _P_CONV_B, _P_BA, _P_BX, _P_LAM, _P_RG_G, _P_LB, _P_HG_G = 0, 4, 5, 6, 7, 8, 9, 10
_P_ROWS = 16


def _silu(x):
    return x * jax.nn.sigmoid(x)


def _gelu_tanh(x):
    cdf = 0.5 * (1.0 + jnp.tanh(0.7978845608028654 * (x + 0.044715 * (x * x * x))))
    return x * cdf


def _rmsnorm(x, g):
    return x * lax.rsqrt(jnp.mean(x * x, axis=-1, keepdims=True) + EPS) * g


def _softplus(z):
    return jnp.maximum(z, 0.0) + jnp.log1p(jnp.exp(-jnp.abs(z)))


def _bdot(a, b):
    return jnp.dot(a.astype(BF16), b.astype(BF16), preferred_element_type=F32)


def _rglru_coeffs(xc, wg, p_ref):
    half = W_A // 2
    g0 = _bdot(xc[:, :half], wg[0])
    g1 = _bdot(xc[:, half:], wg[1])
    r = jax.nn.sigmoid(jnp.concatenate([g0[:, :half], g1[:, :half]], axis=1) + p_ref[_P_BA:_P_BA + 1, :])
    i = jax.nn.sigmoid(jnp.concatenate([g0[:, half:], g1[:, half:]], axis=1) + p_ref[_P_BX:_P_BX + 1, :])
    log_a = (-RG_C * _softplus(-p_ref[_P_LAM:_P_LAM + 1, :])) * r
    a = jnp.exp(log_a)
    one_minus_a2 = -jnp.tanh(log_a) * (a * a + 1.0)
    return a, one_minus_a2, i * xc


def _sqrt_nonneg(x):
    return jnp.where(x == 0.0, 0.0, x * lax.rsqrt(x))


def _group_roll(x, k):
    r, w = x.shape
    return pltpu.roll(x.reshape(r // SUBLANES, SUBLANES, w), k, 1).reshape(r, w)


def _scan_rows(a, bb, h0):
    r, w = a.shape
    sub = lax.broadcasted_iota(jnp.int32, (r, w), 0) & (SUBLANES - 1)
    k = 1
    while k < SUBLANES:
        keep = sub >= k
        bb = a * jnp.where(keep, _group_roll(bb, k), 0.0) + bb
        a = a * jnp.where(keep, _group_roll(a, k), 1.0)
        k *= 2
    outs = []
    carry = h0
    for g in range(r // SUBLANES):
        rows = slice(g * SUBLANES, (g + 1) * SUBLANES)
        hg = a[rows] * carry + bb[rows]
        outs.append(hg)
        carry = hg[SUBLANES - 1:SUBLANES, :]
    return jnp.concatenate(outs, axis=0)


def _head_rmsnorm_gate(o, gb, g_row):
    outs = []
    for h in range(HG_HEADS):
        hs = slice(h * HG_D, (h + 1) * HG_D)
        outs.append(_rmsnorm(o[:, hs], g_row[:, hs]))
    return jnp.concatenate(outs, axis=1) * _silu(gb)


def _mod_kernel(c_ref, w_ref, b_ref, o_ref):
    o_ref[...] = _bdot(_silu(c_ref[...]), w_ref[...]) + b_ref[...]


def _modulation(c_all, mod_w, mod_b):
    rows = c_all.shape[0]
    tn = 1536
    n_out = mod_w.shape[-1]
    return pl.pallas_call(
        _mod_kernel,
        grid=(DEPTH, n_out // tn),
        in_specs=[
            pl.BlockSpec((rows, D_MODEL), lambda l, j: (0, 0)),
            pl.BlockSpec((None, D_MODEL, tn), lambda l, j: (l, 0, j)),
            pl.BlockSpec((None, 1, tn), lambda l, j: (l, 0, j)),
        ],
        out_specs=pl.BlockSpec((None, rows, tn), lambda l, j: (l, 0, j)),
        out_shape=jax.ShapeDtypeStruct((DEPTH, rows, n_out), F32),
        compiler_params=pltpu.CompilerParams(
            dimension_semantics=("arbitrary", "arbitrary"), vmem_limit_bytes=VMEM_LIMIT),
        name="adaln_mod",
    )(c_all, mod_w, mod_b.reshape(DEPTH, 1, n_out))


def _lb_kernel(l_ref, o_ref):
    x = l_ref[...]
    e = jnp.exp(x - jnp.max(x, axis=0, keepdims=True))
    p = e / jnp.sum(e, axis=0, keepdims=True)
    p0 = p[0:1, :]
    run = p0
    o_ref[0:1, :] = run - p0
    for r in range(1, DEPTH):
        run = run + p[r:r + 1, :]
        o_ref[r:r + 1, :] = run - p0


def _lower_bounds(hg_lb_logits):
    return pl.pallas_call(
        _lb_kernel,
        out_shape=jax.ShapeDtypeStruct(hg_lb_logits.shape, F32),
        name="hgrn_lower_bounds",
    )(hg_lb_logits)


def _hgrn_chunk(q, k, v, f, logf, st):
    C = q.shape[0]
    G = C // SUBLANES
    rc = lax.broadcasted_iota(jnp.int32, (C, C), 0)
    cc = lax.broadcasted_iota(jnp.int32, (C, C), 1)
    sr = lax.broadcasted_iota(jnp.int32, (C + 2 * G, C), 0)
    sc = lax.broadcasted_iota(jnp.int32, (C + 2 * G, C), 1)
    bound = jnp.where(sr < C, sr + 1,
                      jnp.where(sr < C + G, (sr - C) * SUBLANES, (sr - C - G + 1) * SUBLANES))
    hi = logf.astype(BF16)
    rest = logf - hi.astype(F32)
    mid = rest.astype(BF16)
    lo = (rest - mid.astype(F32)).astype(BF16)
    sums = jnp.dot((sc < bound).astype(BF16), jnp.concatenate([hi, mid, lo], axis=1), preferred_element_type=F32)
    sums = (sums[:, :W_B] + sums[:, W_B:2 * W_B]) + sums[:, 2 * W_B:]
    b, g_start, g_end = sums[:C], sums[C:C + G], sums[C + G:]
    row = lax.broadcasted_iota(jnp.int32, (C, W_B), 0)
    xor = rc ^ cc

    heads = [slice(h * HG_D, (h + 1) * HG_D) for h in range(HG_HEADS)]
    nt = (((1,), (1,)), ((), ()))
    qb, kb = q.astype(BF16), k.astype(BF16)
    att = [jnp.where(rc == cc, lax.dot_general(qb[:, hs], kb[:, hs], nt, preferred_element_type=F32), 0.0)
           for hs in heads]

    def add_level(x, m, lg):
        mask = ((xor >> lg) == 1) & ((rc & m) != 0)
        for h, hs in enumerate(heads):
            p = lax.dot_general(x[:, hs], x[:, hs], nt, preferred_element_type=F32)
            att[h] = jnp.where(mask, p, att[h])

    c = b - logf
    d = b
    m, lg = 1, 0
    while m < SUBLANES:
        odd = (row & m) != 0
        if m == 1:
            x = jnp.where(odd, q * f, k)
        else:
            x = jnp.where(odd, q, k) * jnp.exp(jnp.where(odd, b - c, d - b))
        add_level(x.astype(BF16), m, lg)
        c = jnp.where(odd, _group_roll(c, m), c)
        d = jnp.where(odd, d, _group_roll(d, SUBLANES - m))
        m, lg = 2 * m, lg + 1

    q8 = q * jnp.exp(b - c)
    k8 = k * jnp.exp(d - b)
    grow = lax.broadcasted_iota(jnp.int32, (G, W_B), 0)
    cg, dg = g_start, g_end
    mu = 1
    while mu < G:
        oddg = (grow & mu) != 0
        rg = jnp.exp(jnp.where(oddg, g_start - cg, dg - g_end))
        pieces = []
        for g in range(G):
            base = q8 if (g // mu) % 2 == 1 else k8
            pieces.append(base[g * SUBLANES:(g + 1) * SUBLANES] * rg[g:g + 1, :])
        add_level(jnp.concatenate(pieces, axis=0).astype(BF16), m, lg)
        cg = jnp.where(oddg, pltpu.roll(cg, mu, 0), cg)
        dg = jnp.where(oddg, dg, pltpu.roll(dg, G - mu, 0))
        mu, m, lg = 2 * mu, 2 * m, lg + 1

    b_last = b[C - 1:C, :]
    qe = (q * jnp.exp(b)).astype(BF16)
    kd = (k * jnp.exp(b_last - b)).astype(BF16)
    vb = v.astype(BF16)
    tn = (((0,), (0,)), ((), ()))
    outs, new_st = [], []
    for h, hs in enumerate(heads):
        o_inter = lax.dot_general(qe[:, hs], st[h].astype(BF16), nt, preferred_element_type=F32)
        o_intra = jnp.dot(att[h].astype(BF16), vb[:, hs], preferred_element_type=F32)
        outs.append(o_inter + o_intra)
        upd = lax.dot_general(vb[:, hs], kd[:, hs], tn, preferred_element_type=F32)
        new_st.append(jnp.exp(b_last[:, hs]) * st[h] + upd)
    return jnp.concatenate(outs, axis=1), new_st


def _mix_prompt_kernel(x_ref, g1_ref, sh_ref, sc_ref, gt_ref, win_ref, p_ref, wg_ref, wout_ref,
                       xo_ref, hn_ref, tail_ref, sn_ref,
                       hcar, ext, st_scr, *, hg_chunk):
    t = pl.program_id(1)
    tc = x_ref.shape[0]

    @pl.when(t == 0)
    def _():
        hcar[...] = jnp.zeros_like(hcar)
        ext[0:SUBLANES, :] = jnp.zeros((SUBLANES, W_A), F32)
        st_scr[...] = jnp.zeros_like(st_scr)

    x = x_ref[...]
    h = _rmsnorm(x, g1_ref[...]) * (1.0 + sc_ref[...]) + sh_ref[...]
    u = _bdot(h, win_ref[...])

    xa = u[:, 0:W_A]
    ga = u[:, W_A:2 * W_A]
    ext[SUBLANES:, :] = xa
    xc = p_ref[_P_CONV_B:_P_CONV_B + 1, :]
    for j in range(CONV_W):
        lo = SUBLANES - (CONV_W - 1) + j
        xc = xc + ext[lo:lo + tc, :] * p_ref[_P_CONV_W + j:_P_CONV_W + j + 1, :]
    tail = xa[tc - SUBLANES:, :]
    ext[0:SUBLANES, :] = tail
    tail_ref[...] = tail

    a, one_minus_a2, ix = _rglru_coeffs(xc, wg_ref, p_ref)
    row = lax.broadcasted_iota(jnp.int32, (tc, W_A), 0)
    first = (row == 0) & (t == 0)
    a = jnp.where(first, 0.0, a)
    bb = jnp.where(first, 1.0, _sqrt_nonneg(one_minus_a2)) * ix
    hseq = _scan_rows(a, bb, hcar[...])
    h_last = hseq[tc - 1:tc, :]
    hcar[...] = h_last
    hn_ref[...] = h_last
    y_a = _rmsnorm(hseq * _gelu_tanh(ga), p_ref[_P_RG_G:_P_RG_G + 1, :])

    lb = p_ref[_P_LB:_P_LB + 1, :]
    f = lb + (1.0 - lb) * jax.nn.sigmoid(u[:, 3 * W_A:4 * W_A])
    q = _silu(u[:, 2 * W_A:3 * W_A])
    kk = 1.0 - f
    logf = jnp.log(f)
    v = u[:, 4 * W_A:5 * W_A]
    st = [st_scr[hh] for hh in range(HG_HEADS)]
    outs = []
    for c0 in range(0, tc, hg_chunk):
        cs = slice(c0, c0 + hg_chunk)
        o_c, st = _hgrn_chunk(q[cs], kk[cs], v[cs], f[cs], logf[cs], st)
        outs.append(o_c)
    o = outs[0] if len(outs) == 1 else jnp.concatenate(outs, axis=0)
    for hh in range(HG_HEADS):
        st_scr[hh] = st[hh]
        sn_ref[hh] = st[hh].T
    y_b = _head_rmsnorm_gate(o, u[:, 5 * W_A:6 * W_A], p_ref[_P_HG_G:_P_HG_G + 1, :])

    y = _bdot(jnp.concatenate([y_a, y_b], axis=1), wout_ref[...])
    xo_ref[...] = x + gt_ref[...] * y


def _mix_prompt(x, g1, mod, w_in, pvec, wg, w_out, layer, *, tc=256, hg_chunk=128):
    bsz, seq, _ = x.shape
    const = lambda b, t: (0, 0)
    wconst = lambda b, t: (layer, 0, 0)
    kern = functools.partial(_mix_prompt_kernel, hg_chunk=hg_chunk)
    return pl.pallas_call(
        kern,
        grid=(bsz, seq // tc),
        in_specs=[
            pl.BlockSpec((None, tc, D_MODEL), lambda b, t: (b, t, 0)),
            pl.BlockSpec((1, D_MODEL), const),
            pl.BlockSpec((None, 1, D_MODEL), lambda b, t: (b, 0, 0)),
            pl.BlockSpec((None, 1, D_MODEL), lambda b, t: (b, 0, 1)),
            pl.BlockSpec((None, 1, D_MODEL), lambda b, t: (b, 0, 2)),
            pl.BlockSpec((None, D_MODEL, D_IN), wconst),
            pl.BlockSpec((_P_ROWS, W_A), const),
            pl.BlockSpec((2, W_A // 2, W_A), lambda b, t: (0, 0, 0)),
            pl.BlockSpec((None, D_MODEL, D_MODEL), wconst),
        ],
        out_specs=[
            pl.BlockSpec((None, tc, D_MODEL), lambda b, t: (b, t, 0)),
            pl.BlockSpec((None, 1, W_A), lambda b, t: (b, 0, 0)),
            pl.BlockSpec((None, SUBLANES, W_A), lambda b, t: (b, 0, 0)),
            pl.BlockSpec((None, HG_HEADS, HG_D, HG_D), lambda b, t: (b, 0, 0, 0)),
        ],
        out_shape=[
            jax.ShapeDtypeStruct(x.shape, F32),
            jax.ShapeDtypeStruct((bsz, 1, W_A), F32),
            jax.ShapeDtypeStruct((bsz, SUBLANES, W_A), F32),
            jax.ShapeDtypeStruct((bsz, HG_HEADS, HG_D, HG_D), F32),
        ],
        scratch_shapes=[
            pltpu.VMEM((1, W_A), F32),
            pltpu.VMEM((tc + SUBLANES, W_A), F32),
            pltpu.VMEM((HG_HEADS, HG_D, HG_D), F32),
        ],
        compiler_params=pltpu.CompilerParams(
            dimension_semantics=("arbitrary", "arbitrary"), vmem_limit_bytes=VMEM_LIMIT),
        name="mix_prompt",
    )(x, g1, mod, mod, mod, w_in, pvec, wg, w_out)


def _in_sample_kernel(x_ref, g1_ref, sh_ref, sc_ref, win_ref, u_ref):
    h = _rmsnorm(x_ref[...], g1_ref[...]) * (1.0 + sc_ref[...]) + sh_ref[...]
    u_ref[...] = _bdot(h, win_ref[...])


def _in_sample(x, g1, mod, w_in, layer):
    n = x.shape[0]
    tn = 1024
    return pl.pallas_call(
        _in_sample_kernel,
        grid=(D_IN // tn,),
        in_specs=[
            pl.BlockSpec((n, D_MODEL), lambda j: (0, 0)),
            pl.BlockSpec((1, D_MODEL), lambda j: (0, 0)),
            pl.BlockSpec((n, D_MODEL), lambda j: (0, 0)),
            pl.BlockSpec((n, D_MODEL), lambda j: (0, 1)),
            pl.BlockSpec((None, D_MODEL, tn), lambda j: (layer, 0, j)),
        ],
        out_specs=pl.BlockSpec((n, tn), lambda j: (0, j)),
        out_shape=jax.ShapeDtypeStruct((n, D_IN), F32),
        compiler_params=pltpu.CompilerParams(
            dimension_semantics=("arbitrary",), vmem_limit_bytes=VMEM_LIMIT),
        name="in_sample",
    )(x, g1, mod, mod, w_in)


def _mix_sample_kernel(u_ref, h0_ref, c0_ref, s0_ref, p_ref, wg_ref,
                       y_ref, hn_ref, cn_ref, sn_ref, o_scr):
    bs = u_ref.shape[0]
    u = u_ref[...]
    xa = u[:, 0:W_A]
    ga = u[:, W_A:2 * W_A]
    c0 = c0_ref[...]
    xc = p_ref[_P_CONV_B:_P_CONV_B + 1, :]
    for j in range(CONV_W - 1):
        xc = xc + c0[:, j * W_A:(j + 1) * W_A] * p_ref[_P_CONV_W + j:_P_CONV_W + j + 1, :]
    xc = xc + xa * p_ref[_P_CONV_W + CONV_W - 1:_P_CONV_W + CONV_W, :]
    cn_ref[:, 0:2 * W_A] = c0[:, W_A:]
    cn_ref[:, 2 * W_A:] = xa

    a, one_minus_a2, ix = _rglru_coeffs(xc, wg_ref, p_ref)
    hnew = _sqrt_nonneg(one_minus_a2) * ix + a * h0_ref[...]
    hn_ref[...] = hnew
    y_a = _rmsnorm(hnew * _gelu_tanh(ga), p_ref[_P_RG_G:_P_RG_G + 1, :])

    lb = p_ref[_P_LB:_P_LB + 1, :]
    f = lb + (1.0 - lb) * jax.nn.sigmoid(u[:, 3 * W_A:4 * W_A])
    q = _silu(u[:, 2 * W_A:3 * W_A])
    kk = 1.0 - f
    v = u[:, 4 * W_A:5 * W_A]
    eye = (lax.broadcasted_iota(jnp.int32, (bs, bs), 0) == lax.broadcasted_iota(jnp.int32, (bs, bs), 1)).astype(F32)
    cols = lax.dot_general(jnp.concatenate([f, kk, q], axis=1), eye, (((0,), (0,)), ((), ())),
                           precision=HIGHEST, preferred_element_type=F32)
    for j in range(bs):
        for hh in range(HG_HEADS):
            hs = slice(hh * HG_D, (hh + 1) * HG_D)
            fcol = cols[hh * HG_D:(hh + 1) * HG_D, j:j + 1]
            kcol = cols[W_B + hh * HG_D:W_B + (hh + 1) * HG_D, j:j + 1]
            qcol = cols[2 * W_B + hh * HG_D:2 * W_B + (hh + 1) * HG_D, j:j + 1]
            s_new = fcol * s0_ref[j, hh] + kcol * v[j:j + 1, hs]
            sn_ref[j, hh] = s_new
            o_scr[j:j + 1, hs] = jnp.sum(qcol * s_new, axis=0, keepdims=True)
    y_b = _head_rmsnorm_gate(o_scr[...], u[:, 5 * W_A:6 * W_A], p_ref[_P_HG_G:_P_HG_G + 1, :])
    y_ref[:, 0:W_A] = y_a
    y_ref[:, W_A:] = y_b


def _mix_sample(u, h0, c0, s0, pvec, wg, layer, *, bs=8):
    n = u.shape[0]
    const = lambda i: (0, 0)
    return pl.pallas_call(
        _mix_sample_kernel,
        grid=(n // bs,),
        in_specs=[
            pl.BlockSpec((bs, D_IN), lambda i: (i, 0)),
            pl.BlockSpec((None, bs, W_A), lambda i: (layer, i, 0)),
            pl.BlockSpec((None, bs, (CONV_W - 1) * W_A), lambda i: (layer, i, 0)),
            pl.BlockSpec((None, bs, HG_HEADS, HG_D, HG_D), lambda i: (layer, i, 0, 0, 0)),
            pl.BlockSpec((_P_ROWS, W_A), const),
            pl.BlockSpec((2, W_A // 2, W_A), lambda i: (0, 0, 0)),
        ],
        out_specs=[
            pl.BlockSpec((bs, D_MODEL), lambda i: (i, 0)),
            pl.BlockSpec((bs, W_A), lambda i: (i, 0)),
            pl.BlockSpec((bs, (CONV_W - 1) * W_A), lambda i: (i, 0)),
            pl.BlockSpec((bs, HG_HEADS, HG_D, HG_D), lambda i: (i, 0, 0, 0)),
        ],
        out_shape=[
            jax.ShapeDtypeStruct((n, D_MODEL), F32),
            jax.ShapeDtypeStruct((n, W_A), F32),
            jax.ShapeDtypeStruct((n, (CONV_W - 1) * W_A), F32),
            jax.ShapeDtypeStruct((n, HG_HEADS, HG_D, HG_D), F32),
        ],
        scratch_shapes=[pltpu.VMEM((bs, W_B), F32)],
        compiler_params=pltpu.CompilerParams(
            dimension_semantics=("arbitrary",), vmem_limit_bytes=VMEM_LIMIT),
        name="mix_sample",
    )(u, h0, c0, s0, pvec, wg)


def _out_sample_kernel(x_ref, y_ref, gt_ref, wout_ref, xo_ref):
    xo_ref[...] = x_ref[...] + gt_ref[...] * _bdot(y_ref[...], wout_ref[...])


def _out_sample(x, y, mod, w_out, layer):
    n = x.shape[0]
    return pl.pallas_call(
        _out_sample_kernel,
        grid=(1,),
        in_specs=[
            pl.BlockSpec((n, D_MODEL), lambda i: (0, 0)),
            pl.BlockSpec((n, D_MODEL), lambda i: (0, 0)),
            pl.BlockSpec((n, D_MODEL), lambda i: (0, 2)),
            pl.BlockSpec((None, D_MODEL, D_MODEL), lambda i: (layer, 0, 0)),
        ],
        out_specs=pl.BlockSpec((n, D_MODEL), lambda i: (0, 0)),
        out_shape=jax.ShapeDtypeStruct((n, D_MODEL), F32),
        compiler_params=pltpu.CompilerParams(
            dimension_semantics=("arbitrary",), vmem_limit_bytes=VMEM_LIMIT),
        name="out_sample",
    )(x, y, mod, w_out)


def _top2(h, router):
    logits = jnp.dot(h, router, precision=HIGHEST, preferred_element_type=F32)
    lane = lax.broadcasted_iota(jnp.int32, logits.shape, 1).astype(F32)
    neg = -jnp.inf
    lg = jnp.where(lane < N_EXPERTS, logits, neg)
    m1 = jnp.max(lg, axis=1, keepdims=True)
    i1 = jnp.min(jnp.where(lg == m1, lane, float(LANES)), axis=1, keepdims=True)
    lg2 = jnp.where(lane == i1, neg, lg)
    m2 = jnp.max(lg2, axis=1, keepdims=True)
    i2 = jnp.min(jnp.where(lg2 == m2, lane, float(LANES)), axis=1, keepdims=True)
    e2 = jnp.exp(m2 - m1)
    den = 1.0 + e2
    return lane, i1, i2, 1.0 / den, e2 / den


def _top2_gates(h, router):
    lane, i1, i2, w1, w2 = _top2(h, router)
    return jnp.where(lane == i1, w1, 0.0) + jnp.where(lane == i2, w2, 0.0)


def _ffn_kernel(*refs, moe):
    if moe:
        (x_ref, g_ref, sh_ref, sc_ref, gt_ref, r_ref, w1_ref, w3_ref, w2_ref,
         xo_ref, h_scr, acc_scr, gate_scr) = refs
    else:
        (x_ref, g_ref, sh_ref, sc_ref, gt_ref, w1_ref, w3_ref, w2_ref,
         xo_ref, h_scr, acc_scr) = refs
    e = pl.program_id(1)
    j = pl.program_id(2)

    @pl.when((e == 0) & (j == 0))
    def _():
        h = _rmsnorm(x_ref[...], g_ref[...]) * (1.0 + sc_ref[...]) + sh_ref[...]
        h_scr[...] = h.astype(BF16)
        acc_scr[...] = jnp.zeros_like(acc_scr)
        if moe:
            gate_scr[...] = _top2_gates(h, r_ref[...])

    h = h_scr[...]
    a = jnp.dot(h, w1_ref[...].astype(BF16), preferred_element_type=F32)
    b = jnp.dot(h, w3_ref[...].astype(BF16), preferred_element_type=F32)
    part = _bdot(_silu(a) * b, w2_ref[...])
    if moe:
        lane = lax.broadcasted_iota(jnp.int32, gate_scr.shape, 1)
        part = part * jnp.sum(jnp.where(lane == e, gate_scr[...], 0.0), axis=1, keepdims=True)
    acc_scr[...] += part

    @pl.when((e == pl.num_programs(1) - 1) & (j == pl.num_programs(2) - 1))
    def _():
        xo_ref[...] = x_ref[...] + gt_ref[...] * acc_scr[...]


def _ffn(x, g2, mod, mod_spec, w1, w3, w2, wl, router=None, *, tm, tf=1408):
    n = x.shape[0]
    n_exp = w1.shape[1]
    moe = router is not None
    in_specs = [
        pl.BlockSpec((tm, D_MODEL), lambda i, e, j: (i, 0)),
        pl.BlockSpec((1, D_MODEL), lambda i, e, j: (0, 0)),
        mod_spec(3), mod_spec(4), mod_spec(5),
    ]
    args = [x, g2, mod, mod, mod]
    scratch = [pltpu.VMEM((tm, D_MODEL), BF16), pltpu.VMEM((tm, D_MODEL), F32)]
    if moe:
        in_specs.append(pl.BlockSpec((D_MODEL, LANES), lambda i, e, j: (0, 0)))
        args.append(router)
        scratch.append(pltpu.VMEM((tm, LANES), F32))
    in_specs += [
        pl.BlockSpec((None, None, D_MODEL, tf), lambda i, e, j: (wl, e, 0, j)),
        pl.BlockSpec((None, None, D_MODEL, tf), lambda i, e, j: (wl, e, 0, j)),
        pl.BlockSpec((None, None, tf, D_MODEL), lambda i, e, j: (wl, e, j, 0)),
    ]
    args += [w1, w3, w2]
    return pl.pallas_call(
        functools.partial(_ffn_kernel, moe=moe),
        grid=(n // tm, n_exp, D_FF // tf),
        in_specs=in_specs,
        out_specs=pl.BlockSpec((tm, D_MODEL), lambda i, e, j: (i, 0)),
        out_shape=jax.ShapeDtypeStruct((n, D_MODEL), F32),
        scratch_shapes=scratch,
        compiler_params=pltpu.CompilerParams(
            dimension_semantics=("arbitrary", "arbitrary", "arbitrary"), vmem_limit_bytes=VMEM_LIMIT),
        name="ffn_moe" if moe else "ffn_dense",
    )(*args)


_R_I1, _R_I2, _R_W1, _R_W2, _R_RANK1, _R_RANK2 = 0, 1, 2, 3, 4, 5
_ROW_DMA_UNROLL = 8


def _route_kernel(x_ref, g_ref, sh_ref, sc_ref, r_ref, h_ref, route_ref, cnt_ref, run_scr):
    i = pl.program_id(0)
    tm = x_ref.shape[0]

    @pl.when(i == 0)
    def _():
        run_scr[...] = jnp.zeros_like(run_scr)

    h = _rmsnorm(x_ref[...], g_ref[...]) * (1.0 + sc_ref[...]) + sh_ref[...]
    h_ref[...] = h
    lane, i_hi, i_lo, w_hi, w_lo = _top2(h, r_ref[...])
    cnt = ((lane == i_hi) | (lane == i_lo)).astype(BF16)
    rr = lax.broadcasted_iota(jnp.int32, (tm, tm), 0)
    cc = lax.broadcasted_iota(jnp.int32, (tm, tm), 1)
    before = jnp.dot((rr > cc).astype(BF16), cnt, preferred_element_type=F32) + run_scr[...]
    rank_hi = jnp.sum(jnp.where(lane == i_hi, before, 0.0), axis=1, keepdims=True)
    rank_lo = jnp.sum(jnp.where(lane == i_lo, before, 0.0), axis=1, keepdims=True)
    run_scr[...] += jnp.sum(cnt.astype(F32), axis=0, keepdims=True)
    cnt_ref[...] = run_scr[...]
    rec = jnp.zeros(lane.shape, F32)
    for ln, val in ((_R_I1, i_hi), (_R_I2, i_lo), (_R_W1, w_hi), (_R_W2, w_lo),
                    (_R_RANK1, rank_hi), (_R_RANK2, rank_lo)):
        rec = jnp.where(lane == float(ln), val, rec)
    route_ref[...] = rec


def _route(x, g2, mod, mod_spec, router, *, tm):
    n = x.shape[0]
    return pl.pallas_call(
        _route_kernel,
        grid=(n // tm,),
        in_specs=[
            pl.BlockSpec((tm, D_MODEL), lambda i: (i, 0)),
            pl.BlockSpec((1, D_MODEL), lambda i: (0, 0)),
            mod_spec(3), mod_spec(4),
            pl.BlockSpec((D_MODEL, LANES), lambda i: (0, 0)),
        ],
        out_specs=[
            pl.BlockSpec((tm, D_MODEL), lambda i: (i, 0)),
            pl.BlockSpec((tm, LANES), lambda i: (i, 0)),
            pl.BlockSpec((1, LANES), lambda i: (0, 0)),
        ],
        out_shape=[
            jax.ShapeDtypeStruct((n, D_MODEL), F32),
            jax.ShapeDtypeStruct((n, LANES), F32),
            jax.ShapeDtypeStruct((1, LANES), F32),
        ],
        scratch_shapes=[pltpu.VMEM((1, LANES), F32)],
        compiler_params=pltpu.CompilerParams(
            dimension_semantics=("arbitrary",), vmem_limit_bytes=VMEM_LIMIT),
        name="moe_route",
    )(x, g2, mod, mod, router)


def _dispatch_kernel(pos_ref, h_ref, xs_in_ref, xs_ref, sem):
    del xs_in_ref
    rows = h_ref.shape[0]

    def row_copy(r, k):
        return pltpu.make_async_copy(
            h_ref.at[pl.ds(r, 1), :], xs_ref.at[pl.ds(pos_ref[2 * r + k], 1), :], sem)

    def start(r, carry):
        row_copy(r, 0).start(priority=0)
        row_copy(r, 1).start(priority=1)
        return carry

    def wait(r, carry):
        row_copy(r, 0).wait()
        row_copy(r, 1).wait()
        return carry

    lax.fori_loop(0, rows, start, 0, unroll=_ROW_DMA_UNROLL)
    lax.fori_loop(0, rows, wait, 0, unroll=_ROW_DMA_UNROLL)


def _dispatch(h, pos, n_rows, *, tm):
    n = h.shape[0]
    xs0 = jnp.zeros((n_rows, D_MODEL), F32)
    return pl.pallas_call(
        _dispatch_kernel,
        grid=(n // tm,),
        in_specs=[
            pl.BlockSpec((2 * tm,), lambda i: (i,), memory_space=pltpu.SMEM),
            pl.BlockSpec((tm, D_MODEL), lambda i: (i, 0)),
            pl.BlockSpec(memory_space=pl.ANY),
        ],
        out_specs=pl.BlockSpec(memory_space=pl.ANY),
        out_shape=jax.ShapeDtypeStruct((n_rows, D_MODEL), F32),
        scratch_shapes=[pltpu.SemaphoreType.DMA(())],
        input_output_aliases={2: 0},
        compiler_params=pltpu.CompilerParams(
            dimension_semantics=("arbitrary",), vmem_limit_bytes=VMEM_LIMIT),
        name="moe_dispatch",
    )(pos, h, xs0)


def _gffn_kernel(te_ref, nv_ref, xs_ref, w1_ref, w3_ref, w2_ref, ys_ref, h_scr, acc_scr):
    del te_ref
    i = pl.program_id(0)
    j = pl.program_id(1)

    @pl.when(i < nv_ref[0])
    def _():
        @pl.when(j == 0)
        def _():
            h_scr[...] = xs_ref[...].astype(BF16)
            acc_scr[...] = jnp.zeros_like(acc_scr)

        h = h_scr[...]
        a = jnp.dot(h, w1_ref[...].astype(BF16), preferred_element_type=F32)
        b = jnp.dot(h, w3_ref[...].astype(BF16), preferred_element_type=F32)
        acc_scr[...] += _bdot(_silu(a) * b, w2_ref[...])

        @pl.when(j == pl.num_programs(1) - 1)
        def _():
            ys_ref[...] = acc_scr[...]

    @pl.when((i >= nv_ref[0]) & (j == 0))
    def _():
        ys_ref[...] = jnp.zeros_like(ys_ref)


def _gffn(xs, tile_expert, n_valid, w1, w3, w2, wl, *, tm, tf=1408):
    n_rows = xs.shape[0]
    last_j = D_FF // tf - 1

    def row_idx(i, j, te, nv):
        return (jnp.minimum(i, nv[0] - 1), 0)

    def ff_idx(i, j, nv):
        return jnp.where(i < nv[0], j, last_j)

    grid_spec = pltpu.PrefetchScalarGridSpec(
        num_scalar_prefetch=2,
        grid=(n_rows // tm, D_FF // tf),
        in_specs=[
            pl.BlockSpec((tm, D_MODEL), row_idx),
            pl.BlockSpec((None, None, D_MODEL, tf), lambda i, j, te, nv: (wl, te[i], 0, ff_idx(i, j, nv))),
            pl.BlockSpec((None, None, D_MODEL, tf), lambda i, j, te, nv: (wl, te[i], 0, ff_idx(i, j, nv))),
            pl.BlockSpec((None, None, tf, D_MODEL), lambda i, j, te, nv: (wl, te[i], ff_idx(i, j, nv), 0)),
        ],
        out_specs=pl.BlockSpec((tm, D_MODEL), lambda i, j, te, nv: (i, 0)),
        scratch_shapes=[pltpu.VMEM((tm, D_MODEL), BF16), pltpu.VMEM((tm, D_MODEL), F32)],
    )
    return pl.pallas_call(
        _gffn_kernel,
        grid_spec=grid_spec,
        out_shape=jax.ShapeDtypeStruct((n_rows, D_MODEL), F32),
        compiler_params=pltpu.CompilerParams(
            dimension_semantics=("arbitrary", "arbitrary"), vmem_limit_bytes=VMEM_LIMIT),
        name="moe_grouped_ffn",
    )(tile_expert, n_valid, xs, w1, w3, w2)


def _combine_kernel(pos_ref, x_ref, gt_ref, route_ref, fg_ref, ys_ref, xo_ref, y0_scr, y1_scr, sem, *, final_norm):
    rows = x_ref.shape[0]

    def row_copy(r, k):
        buf = y0_scr if k == 0 else y1_scr
        return pltpu.make_async_copy(
            ys_ref.at[pl.ds(pos_ref[2 * r + k], 1), :], buf.at[pl.ds(r, 1), :], sem)

    def start(r, carry):
        row_copy(r, 0).start(priority=0)
        row_copy(r, 1).start(priority=1)
        return carry

    def wait(r, carry):
        row_copy(r, 0).wait()
        row_copy(r, 1).wait()
        return carry

    lax.fori_loop(0, rows, start, 0, unroll=_ROW_DMA_UNROLL)
    lax.fori_loop(0, rows, wait, 0, unroll=_ROW_DMA_UNROLL)
    route = route_ref[...]
    fo = route[:, _R_W1:_R_W1 + 1] * y0_scr[...] + route[:, _R_W2:_R_W2 + 1] * y1_scr[...]
    x_new = x_ref[...] + gt_ref[...] * fo
    xo_ref[...] = _rmsnorm(x_new, fg_ref[...]) if final_norm else x_new


def _combine(x, mod, mod_spec, route, pos, ys, final_g, *, tm):
    n = x.shape[0]
    final_norm = final_g is not None
    fg = final_g if final_norm else jnp.ones((1, D_MODEL), F32)
    return pl.pallas_call(
        functools.partial(_combine_kernel, final_norm=final_norm),
        grid=(n // tm,),
        in_specs=[
            pl.BlockSpec((2 * tm,), lambda i: (i,), memory_space=pltpu.SMEM),
            pl.BlockSpec((tm, D_MODEL), lambda i: (i, 0)),
            mod_spec(5),
            pl.BlockSpec((tm, LANES), lambda i: (i, 0)),
            pl.BlockSpec((1, D_MODEL), lambda i: (0, 0)),
            pl.BlockSpec(memory_space=pl.ANY),
        ],
        out_specs=pl.BlockSpec((tm, D_MODEL), lambda i: (i, 0)),
        out_shape=jax.ShapeDtypeStruct((n, D_MODEL), F32),
        scratch_shapes=[pltpu.VMEM((tm, D_MODEL), F32), pltpu.VMEM((tm, D_MODEL), F32),
                        pltpu.SemaphoreType.DMA(())],
        compiler_params=pltpu.CompilerParams(
            dimension_semantics=("arbitrary",), vmem_limit_bytes=VMEM_LIMIT),
        name="moe_combine",
    )(pos, x, mod, route, fg, ys)


def _moe_routed(x, g2, mod, tiles_per_seq_of, router, w1, w3, w2, wl, final_g=None, *,
                tm_route=512, tm_move=256, tm_group=512):
    n = x.shape[0]

    def spec1(tm):
        return lambda c: pl.BlockSpec((None, 1, D_MODEL), lambda i: (i // tiles_per_seq_of(tm), 0, c))

    h, route, counts = _route(x, g2, mod, spec1(tm_route), router, tm=tm_route)
    cnt = counts[0, :N_EXPERTS].astype(jnp.int32)
    tiles_e = (cnt + tm_group - 1) // tm_group
    tile_start = jnp.cumsum(tiles_e) - tiles_e
    n_valid = jnp.sum(tiles_e)
    n_tiles = (2 * n) // tm_group + N_EXPERTS
    tile_ids = jnp.minimum(jnp.arange(n_tiles, dtype=jnp.int32), n_valid - 1)
    tile_expert = (jnp.sum(tile_ids[:, None] >= tile_start[None, :], axis=1) - 1).astype(jnp.int32)
    idx = route[:, _R_I1:_R_I2 + 1].astype(jnp.int32)
    rank = route[:, _R_RANK1:_R_RANK2 + 1].astype(jnp.int32)
    pos = ((tile_start * tm_group)[idx] + rank).reshape(2 * n)

    xs = _dispatch(h, pos, n_tiles * tm_group, tm=tm_move)
    ys = _gffn(xs, tile_expert, n_valid.reshape(1).astype(jnp.int32), w1, w3, w2, wl, tm=tm_group)
    return _combine(x, mod, spec1(tm_move), route, pos, ys, final_g, tm=tm_move)


def _final_norm_kernel(x_ref, g_ref, o_ref):
    o_ref[...] = _rmsnorm(x_ref[...], g_ref[...])


def _final_norm(x, g, *, tm):
    n = x.shape[0]
    return pl.pallas_call(
        _final_norm_kernel,
        grid=(n // tm,),
        in_specs=[pl.BlockSpec((tm, D_MODEL), lambda i: (i, 0)), pl.BlockSpec((1, D_MODEL), lambda i: (0, 0))],
        out_specs=pl.BlockSpec((tm, D_MODEL), lambda i: (i, 0)),
        out_shape=jax.ShapeDtypeStruct((n, D_MODEL), F32),
        compiler_params=pltpu.CompilerParams(
            dimension_semantics=("arbitrary",), vmem_limit_bytes=VMEM_LIMIT),
        name="final_norm",
    )(x, g)


def _block_diag(w):
    nb, bw, _ = w.shape
    eye = jnp.eye(nb, dtype=w.dtype)
    return (eye[:, None, :, None] * w[:, :, None, :]).reshape(nb * bw, nb * bw)


def kernel(x_prompt, x_sample, c_prompt, c_sample, state_rglru_h, state_rglru_conv, state_hgrn,
           mod_w, mod_b, norm1_g, norm2_g, w_in, conv_w, conv_b, rg_wa, rg_ba, rg_wx, rg_bx,
           rg_lambda, rg_out_g, hg_lb_logits, hg_norm_g, w_out, ffn_w1, ffn_w3, ffn_w2,
           router_w, moe_w1, moe_w3, moe_w2, final_g):
    n_b, seq, _ = x_prompt.shape
    n_s = x_sample.shape[0]
    tm_p = 512
    tiles_per_seq = seq // tm_p

    mod = _modulation(jnp.concatenate([c_prompt, c_sample], axis=0), mod_w, mod_b)
    lbs = _lower_bounds(hg_lb_logits)

    def row(v):
        return v.reshape(1, -1)

    w_in_b = w_in.astype(BF16)
    w_out_b = w_out.astype(BF16)
    dense_w = [w.astype(BF16)[:, None] for w in (ffn_w1, ffn_w3, ffn_w2)]
    moe_w = [w.astype(BF16) for w in (moe_w1, moe_w3, moe_w2)]
    conv_all = state_rglru_conv.reshape(DEPTH, n_s, (CONV_W - 1) * W_A)

    xp = x_prompt
    xs = x_sample.reshape(n_s, D_MODEL)
    outs = {k: [] for k in ("hp", "cp", "sp", "hs", "cs", "ss")}
    for l in range(DEPTH):
        mod_p = mod[l, :n_b].reshape(n_b, 1, 6 * D_MODEL)
        mod_s = mod[l, n_b:]
        pvec = jnp.concatenate([
            conv_w[l], row(conv_b[l]), row(rg_ba[l]), row(rg_bx[l]), row(rg_lambda[l]), row(rg_out_g[l]),
            row(lbs[l]), row(hg_norm_g[l]), jnp.zeros((_P_ROWS - 11, W_A), F32)], axis=0)
        bd_a, bd_x, half = _block_diag(rg_wa[l]), _block_diag(rg_wx[l]), W_A // 2
        wg = jnp.stack([
            jnp.concatenate([bd[hh * half:(hh + 1) * half, hh * half:(hh + 1) * half] for bd in (bd_a, bd_x)], axis=1)
            for hh in range(2)]).astype(BF16)

        xp, hp, tail, sp = _mix_prompt(xp, row(norm1_g[l]), mod_p, w_in_b, pvec, wg, w_out_b, l)
        outs["hp"].append(hp.reshape(n_b, W_A))
        outs["cp"].append(tail[:, SUBLANES - (CONV_W - 1):, :])
        outs["sp"].append(sp)

        u_s = _in_sample(xs, row(norm1_g[l]), mod_s, w_in_b, l)
        y_s, hs, cs, ss = _mix_sample(u_s, state_rglru_h, conv_all, state_hgrn, pvec, wg, l)
        xs = _out_sample(xs, y_s, mod_s, w_out_b, l)
        outs["hs"].append(hs)
        outs["cs"].append(cs.reshape(n_s, CONV_W - 1, W_A))
        outs["ss"].append(ss)

        if l % 2 == 0:
            w1, w3, w2 = dense_w
            router = None
        else:
            w1, w3, w2 = moe_w
            router = jnp.pad(router_w[l // 2], ((0, 0), (0, LANES - N_EXPERTS)))
        spec_p = lambda c: pl.BlockSpec((None, 1, D_MODEL), lambda i, e, j: (i // tiles_per_seq, 0, c))
        spec_s = lambda c: pl.BlockSpec((n_s, D_MODEL), lambda i, e, j: (0, c))
        if router is None:
            xp = _ffn(xp.reshape(n_b * seq, D_MODEL), row(norm2_g[l]), mod_p, spec_p, w1, w3, w2, l // 2, None,
                      tm=tm_p)
        else:
            xp = _moe_routed(xp.reshape(n_b * seq, D_MODEL), row(norm2_g[l]), mod_p, lambda tm: seq // tm,
                             router, w1, w3, w2, l // 2, row(final_g) if l == DEPTH - 1 else None)
        xp = xp.reshape(n_b, seq, D_MODEL)
        xs = _ffn(xs, row(norm2_g[l]), mod_s, spec_s, w1, w3, w2, l // 2, router, tm=n_s)

    y_prompt = xp
    y_sample = _final_norm(xs, row(final_g), tm=n_s).reshape(n_s, 1, D_MODEL)
    return (y_prompt, y_sample,
            jnp.stack(outs["hp"]), jnp.stack(outs["cp"]), jnp.stack(outs["sp"]),
            jnp.stack(outs["hs"]), jnp.stack(outs["cs"]), jnp.stack(outs["ss"]))
```

```python
import functools

import jax
import jax.numpy as jnp
from jax import lax
from jax.experimental import pallas as pl
from jax.experimental.pallas import tpu as pltpu

F32 = jnp.float32
BF16 = jnp.bfloat16
HIGHEST = lax.Precision.HIGHEST

D_MODEL = 1024
DEPTH = 4
W_A = 512
RG_BLOCKS = 8
CONV_W = 4
RG_C = 8.0
W_B = 512
HG_HEADS = 4
HG_D = 128
D_IN = 3072
D_FF = 2816
N_EXPERTS = 8
EPS = 1e-6
PAST_LEN = 16384

LANES = 128
SUBLANES = 8
VMEM_LIMIT = 56 * 1024 * 1024

_P_CONV_W, _P_CONV_B, _P_BA, _P_BX, _P_LAM, _P_RG_G, _P_LB, _P_HG_G = 0, 4, 5, 6, 7, 8, 9, 10
_P_ROWS = 16


def _silu(x):
    return x * jax.nn.sigmoid(x)


def _gelu_tanh(x):
    cdf = 0.5 * (1.0 + jnp.tanh(0.7978845608028654 * (x + 0.044715 * (x * x * x))))
    return x * cdf


def _rmsnorm(x, g):
    return x * lax.rsqrt(jnp.mean(x * x, axis=-1, keepdims=True) + EPS) * g


def _softplus(z):
    return jnp.maximum(z, 0.0) + jnp.log1p(jnp.exp(-jnp.abs(z)))


def _bdot(a, b):
    return jnp.dot(a.astype(BF16), b.astype(BF16), preferred_element_type=F32)


def _rglru_coeffs(xc, wg, p_ref):
    half = W_A // 2
    g0 = _bdot(xc[:, :half], wg[0])
    g1 = _bdot(xc[:, half:], wg[1])
    r = jax.nn.sigmoid(jnp.concatenate([g0[:, :half], g1[:, :half]], axis=1) + p_ref[_P_BA:_P_BA + 1, :])
    i = jax.nn.sigmoid(jnp.concatenate([g0[:, half:], g1[:, half:]], axis=1) + p_ref[_P_BX:_P_BX + 1, :])
    log_a = (-RG_C * _softplus(-p_ref[_P_LAM:_P_LAM + 1, :])) * r
    a = jnp.exp(log_a)
    one_minus_a2 = -jnp.tanh(log_a) * (a * a + 1.0)
    return a, one_minus_a2, i * xc


def _sqrt_nonneg(x):
    return jnp.where(x == 0.0, 0.0, x * lax.rsqrt(x))


def _group_roll(x, k):
    r, w = x.shape
    return pltpu.roll(x.reshape(r // SUBLANES, SUBLANES, w), k, 1).reshape(r, w)


def _scan_rows(a, bb, h0):
    r, w = a.shape
    sub = lax.broadcasted_iota(jnp.int32, (r, w), 0) & (SUBLANES - 1)
    k = 1
    while k < SUBLANES:
        keep = sub >= k
        bb = a * jnp.where(keep, _group_roll(bb, k), 0.0) + bb
        a = a * jnp.where(keep, _group_roll(a, k), 1.0)
        k *= 2
    outs = []
    carry = h0
    for g in range(r // SUBLANES):
        rows = slice(g * SUBLANES, (g + 1) * SUBLANES)
        hg = a[rows] * carry + bb[rows]
        outs.append(hg)
        carry = hg[SUBLANES - 1:SUBLANES, :]
    return jnp.concatenate(outs, axis=0)


def _head_rmsnorm_gate(o, gb, g_row):
    outs = []
    for h in range(HG_HEADS):
        hs = slice(h * HG_D, (h + 1) * HG_D)
        outs.append(_rmsnorm(o[:, hs], g_row[:, hs]))
    return jnp.concatenate(outs, axis=1) * _silu(gb)


def _mod_kernel(c_ref, w_ref, b_ref, o_ref):
    o_ref[...] = _bdot(_silu(c_ref[...]), w_ref[...]) + b_ref[...]


def _modulation(c_all, mod_w, mod_b):
    rows = c_all.shape[0]
    tn = 1536
    n_out = mod_w.shape[-1]
    return pl.pallas_call(
        _mod_kernel,
        grid=(DEPTH, n_out // tn),
        in_specs=[
            pl.BlockSpec((rows, D_MODEL), lambda l, j: (0, 0)),
            pl.BlockSpec((None, D_MODEL, tn), lambda l, j: (l, 0, j)),
            pl.BlockSpec((None, 1, tn), lambda l, j: (l, 0, j)),
        ],
        out_specs=pl.BlockSpec((None, rows, tn), lambda l, j: (l, 0, j)),
        out_shape=jax.ShapeDtypeStruct((DEPTH, rows, n_out), F32),
        compiler_params=pltpu.CompilerParams(
            dimension_semantics=("arbitrary", "arbitrary"), vmem_limit_bytes=VMEM_LIMIT),
        name="adaln_mod",
    )(c_all, mod_w, mod_b.reshape(DEPTH, 1, n_out))


def _lb_kernel(l_ref, o_ref):
    x = l_ref[...]
    e = jnp.exp(x - jnp.max(x, axis=0, keepdims=True))
    p = e / jnp.sum(e, axis=0, keepdims=True)
    p0 = p[0:1, :]
    run = p0
    o_ref[0:1, :] = run - p0
    for r in range(1, DEPTH):
        run = run + p[r:r + 1, :]
        o_ref[r:r + 1, :] = run - p0


def _lower_bounds(hg_lb_logits):
    return pl.pallas_call(
        _lb_kernel,
        out_shape=jax.ShapeDtypeStruct(hg_lb_logits.shape, F32),
        name="hgrn_lower_bounds",
    )(hg_lb_logits)


def _hgrn_chunk(q, k, v, f, logf, st):
    C = q.shape[0]
    G = C // SUBLANES
    rc = lax.broadcasted_iota(jnp.int32, (C, C), 0)
    cc = lax.broadcasted_iota(jnp.int32, (C, C), 1)
    sr = lax.broadcasted_iota(jnp.int32, (C + 2 * G, C), 0)
    sc = lax.broadcasted_iota(jnp.int32, (C + 2 * G, C), 1)
    bound = jnp.where(sr < C, sr + 1,
                      jnp.where(sr < C + G, (sr - C) * SUBLANES, (sr - C - G + 1) * SUBLANES))
    hi = logf.astype(BF16)
    rest = logf - hi.astype(F32)
    mid = rest.astype(BF16)
    lo = (rest - mid.astype(F32)).astype(BF16)
    sums = jnp.dot((sc < bound).astype(BF16), jnp.concatenate([hi, mid, lo], axis=1), preferred_element_type=F32)
    sums = (sums[:, :W_B] + sums[:, W_B:2 * W_B]) + sums[:, 2 * W_B:]
    b, g_start, g_end = sums[:C], sums[C:C + G], sums[C + G:]
    row = lax.broadcasted_iota(jnp.int32, (C, W_B), 0)
    xor = rc ^ cc

    heads = [slice(h * HG_D, (h + 1) * HG_D) for h in range(HG_HEADS)]
    nt = (((1,), (1,)), ((), ()))
    qb, kb = q.astype(BF16), k.astype(BF16)
    att = [jnp.where(rc == cc, lax.dot_general(qb[:, hs], kb[:, hs], nt, preferred_element_type=F32), 0.0)
           for hs in heads]

    def add_level(x, m, lg):
        mask = ((xor >> lg) == 1) & ((rc & m) != 0)
        for h, hs in enumerate(heads):
            p = lax.dot_general(x[:, hs], x[:, hs], nt, preferred_element_type=F32)
            att[h] = jnp.where(mask, p, att[h])

    c = b - logf
    d = b
    m, lg = 1, 0
    while m < SUBLANES:
        odd = (row & m) != 0
        if m == 1:
            x = jnp.where(odd, q * f, k)
        else:
            x = jnp.where(odd, q, k) * jnp.exp(jnp.where(odd, b - c, d - b))
        add_level(x.astype(BF16), m, lg)
        c = jnp.where(odd, _group_roll(c, m), c)
        d = jnp.where(odd, d, _group_roll(d, SUBLANES - m))
        m, lg = 2 * m, lg + 1

    q8 = q * jnp.exp(b - c)
    k8 = k * jnp.exp(d - b)
    grow = lax.broadcasted_iota(jnp.int32, (G, W_B), 0)
    cg, dg = g_start, g_end
    mu = 1
    while mu < G:
        oddg = (grow & mu) != 0
        rg = jnp.exp(jnp.where(oddg, g_start - cg, dg - g_end))
        pieces = []
        for g in range(G):
            base = q8 if (g // mu) % 2 == 1 else k8
            pieces.append(base[g * SUBLANES:(g + 1) * SUBLANES] * rg[g:g + 1, :])
        add_level(jnp.concatenate(pieces, axis=0).astype(BF16), m, lg)
        cg = jnp.where(oddg, pltpu.roll(cg, mu, 0), cg)
        dg = jnp.where(oddg, dg, pltpu.roll(dg, G - mu, 0))
        mu, m, lg = 2 * mu, 2 * m, lg + 1

    b_last = b[C - 1:C, :]
    qe = (q * jnp.exp(b)).astype(BF16)
    kd = (k * jnp.exp(b_last - b)).astype(BF16)
    vb = v.astype(BF16)
    tn = (((0,), (0,)), ((), ()))
    outs, new_st = [], []
    for h, hs in enumerate(heads):
        o_inter = lax.dot_general(qe[:, hs], st[h].astype(BF16), nt, preferred_element_type=F32)
        o_intra = jnp.dot(att[h].astype(BF16), vb[:, hs], preferred_element_type=F32)
        outs.append(o_inter + o_intra)
        upd = lax.dot_general(vb[:, hs], kd[:, hs], tn, preferred_element_type=F32)
        new_st.append(jnp.exp(b_last[:, hs]) * st[h] + upd)
    return jnp.concatenate(outs, axis=1), new_st


def _mix_prompt_kernel(x_ref, g1_ref, sh_ref, sc_ref, gt_ref, win_ref, p_ref, wg_ref, wout_ref,
                       xo_ref, hn_ref, tail_ref, sn_ref,
                       hcar, ext, st_scr, *, hg_chunk):
    t = pl.program_id(1)
    tc = x_ref.shape[0]

    @pl.when(t == 0)
    def _():
        hcar[...] = jnp.zeros_like(hcar)
        ext[0:SUBLANES, :] = jnp.zeros((SUBLANES, W_A), F32)
        st_scr[...] = jnp.zeros_like(st_scr)

    x = x_ref[...]
    h = _rmsnorm(x, g1_ref[...]) * (1.0 + sc_ref[...]) + sh_ref[...]
    u = _bdot(h, win_ref[...])

    xa = u[:, 0:W_A]
    ga = u[:, W_A:2 * W_A]
    ext[SUBLANES:, :] = xa
    xc = p_ref[_P_CONV_B:_P_CONV_B + 1, :]
    for j in range(CONV_W):
        lo = SUBLANES - (CONV_W - 1) + j
        xc = xc + ext[lo:lo + tc, :] * p_ref[_P_CONV_W + j:_P_CONV_W + j + 1, :]
    tail = xa[tc - SUBLANES:, :]
    ext[0:SUBLANES, :] = tail
    tail_ref[...] = tail

    a, one_minus_a2, ix = _rglru_coeffs(xc, wg_ref, p_ref)
    row = lax.broadcasted_iota(jnp.int32, (tc, W_A), 0)
    first = (row == 0) & (t == 0)
    a = jnp.where(first, 0.0, a)
    bb = jnp.where(first, 1.0, _sqrt_nonneg(one_minus_a2)) * ix
    hseq = _scan_rows(a, bb, hcar[...])
    h_last = hseq[tc - 1:tc, :]
    hcar[...] = h_last
    hn_ref[...] = h_last
    y_a = _rmsnorm(hseq * _gelu_tanh(ga), p_ref[_P_RG_G:_P_RG_G + 1, :])

    lb = p_ref[_P_LB:_P_LB + 1, :]
    f = lb + (1.0 - lb) * jax.nn.sigmoid(u[:, 3 * W_A:4 * W_A])
    q = _silu(u[:, 2 * W_A:3 * W_A])
    kk = 1.0 - f
    logf = jnp.log(f)
    v = u[:, 4 * W_A:5 * W_A]
    st = [st_scr[hh] for hh in range(HG_HEADS)]
    outs = []
    for c0 in range(0, tc, hg_chunk):
        cs = slice(c0, c0 + hg_chunk)
        o_c, st = _hgrn_chunk(q[cs], kk[cs], v[cs], f[cs], logf[cs], st)
        outs.append(o_c)
    o = outs[0] if len(outs) == 1 else jnp.concatenate(outs, axis=0)
    for hh in range(HG_HEADS):
        st_scr[hh] = st[hh]
        sn_ref[hh] = st[hh].T
    y_b = _head_rmsnorm_gate(o, u[:, 5 * W_A:6 * W_A], p_ref[_P_HG_G:_P_HG_G + 1, :])

    y = _bdot(jnp.concatenate([y_a, y_b], axis=1), wout_ref[...])
    xo_ref[...] = x + gt_ref[...] * y


def _mix_prompt(x, g1, mod, w_in, pvec, wg, w_out, layer, *, tc=256, hg_chunk=128):
    bsz, seq, _ = x.shape
    const = lambda b, t: (0, 0)
    wconst = lambda b, t: (layer, 0, 0)
    kern = functools.partial(_mix_prompt_kernel, hg_chunk=hg_chunk)
    return pl.pallas_call(
        kern,
        grid=(bsz, seq // tc),
        in_specs=[
            pl.BlockSpec((None, tc, D_MODEL), lambda b, t: (b, t, 0)),
            pl.BlockSpec((1, D_MODEL), const),
            pl.BlockSpec((None, 1, D_MODEL), lambda b, t: (b, 0, 0)),
            pl.BlockSpec((None, 1, D_MODEL), lambda b, t: (b, 0, 1)),
            pl.BlockSpec((None, 1, D_MODEL), lambda b, t: (b, 0, 2)),
            pl.BlockSpec((None, D_MODEL, D_IN), wconst),
            pl.BlockSpec((_P_ROWS, W_A), const),
            pl.BlockSpec((2, W_A // 2, W_A), lambda b, t: (0, 0, 0)),
            pl.BlockSpec((None, D_MODEL, D_MODEL), wconst),
        ],
        out_specs=[
            pl.BlockSpec((None, tc, D_MODEL), lambda b, t: (b, t, 0)),
            pl.BlockSpec((None, 1, W_A), lambda b, t: (b, 0, 0)),
            pl.BlockSpec((None, SUBLANES, W_A), lambda b, t: (b, 0, 0)),
            pl.BlockSpec((None, HG_HEADS, HG_D, HG_D), lambda b, t: (b, 0, 0, 0)),
        ],
        out_shape=[
            jax.ShapeDtypeStruct(x.shape, F32),
            jax.ShapeDtypeStruct((bsz, 1, W_A), F32),
            jax.ShapeDtypeStruct((bsz, SUBLANES, W_A), F32),
            jax.ShapeDtypeStruct((bsz, HG_HEADS, HG_D, HG_D), F32),
        ],
        scratch_shapes=[
            pltpu.VMEM((1, W_A), F32),
            pltpu.VMEM((tc + SUBLANES, W_A), F32),
            pltpu.VMEM((HG_HEADS, HG_D, HG_D), F32),
        ],
        compiler_params=pltpu.CompilerParams(
            dimension_semantics=("arbitrary", "arbitrary"), vmem_limit_bytes=VMEM_LIMIT),
        name="mix_prompt",
    )(x, g1, mod, mod, mod, w_in, pvec, wg, w_out)


def _in_sample_kernel(x_ref, g1_ref, sh_ref, sc_ref, win_ref, u_ref):
    h = _rmsnorm(x_ref[...], g1_ref[...]) * (1.0 + sc_ref[...]) + sh_ref[...]
    u_ref[...] = _bdot(h, win_ref[...])


def _in_sample(x, g1, mod, w_in, layer):
    n = x.shape[0]
    tn = 1024
    return pl.pallas_call(
        _in_sample_kernel,
        grid=(D_IN // tn,),
        in_specs=[
            pl.BlockSpec((n, D_MODEL), lambda j: (0, 0)),
            pl.BlockSpec((1, D_MODEL), lambda j: (0, 0)),
            pl.BlockSpec((n, D_MODEL), lambda j: (0, 0)),
            pl.BlockSpec((n, D_MODEL), lambda j: (0, 1)),
            pl.BlockSpec((None, D_MODEL, tn), lambda j: (layer, 0, j)),
        ],
        out_specs=pl.BlockSpec((n, tn), lambda j: (0, j)),
        out_shape=jax.ShapeDtypeStruct((n, D_IN), F32),
        compiler_params=pltpu.CompilerParams(
            dimension_semantics=("arbitrary",), vmem_limit_bytes=VMEM_LIMIT),
        name="in_sample",
    )(x, g1, mod, mod, w_in)


def _mix_sample_kernel(u_ref, h0_ref, c0_ref, s0_ref, p_ref, wg_ref,
                       y_ref, hn_ref, cn_ref, sn_ref, o_scr):
    bs = u_ref.shape[0]
    u = u_ref[...]
    xa = u[:, 0:W_A]
    ga = u[:, W_A:2 * W_A]
    c0 = c0_ref[...]
    xc = p_ref[_P_CONV_B:_P_CONV_B + 1, :]
    for j in range(CONV_W - 1):
        xc = xc + c0[:, j * W_A:(j + 1) * W_A] * p_ref[_P_CONV_W + j:_P_CONV_W + j + 1, :]
    xc = xc + xa * p_ref[_P_CONV_W + CONV_W - 1:_P_CONV_W + CONV_W, :]
    cn_ref[:, 0:2 * W_A] = c0[:, W_A:]
    cn_ref[:, 2 * W_A:] = xa

    a, one_minus_a2, ix = _rglru_coeffs(xc, wg_ref, p_ref)
    hnew = _sqrt_nonneg(one_minus_a2) * ix + a * h0_ref[...]
    hn_ref[...] = hnew
    y_a = _rmsnorm(hnew * _gelu_tanh(ga), p_ref[_P_RG_G:_P_RG_G + 1, :])

    lb = p_ref[_P_LB:_P_LB + 1, :]
    f = lb + (1.0 - lb) * jax.nn.sigmoid(u[:, 3 * W_A:4 * W_A])
    q = _silu(u[:, 2 * W_A:3 * W_A])
    kk = 1.0 - f
    v = u[:, 4 * W_A:5 * W_A]
    eye = (lax.broadcasted_iota(jnp.int32, (bs, bs), 0) == lax.broadcasted_iota(jnp.int32, (bs, bs), 1)).astype(F32)
    cols = lax.dot_general(jnp.concatenate([f, kk, q], axis=1), eye, (((0,), (0,)), ((), ())),
                           precision=HIGHEST, preferred_element_type=F32)
    for j in range(bs):
        for hh in range(HG_HEADS):
            hs = slice(hh * HG_D, (hh + 1) * HG_D)
            fcol = cols[hh * HG_D:(hh + 1) * HG_D, j:j + 1]
            kcol = cols[W_B + hh * HG_D:W_B + (hh + 1) * HG_D, j:j + 1]
            qcol = cols[2 * W_B + hh * HG_D:2 * W_B + (hh + 1) * HG_D, j:j + 1]
            s_new = fcol * s0_ref[j, hh] + kcol * v[j:j + 1, hs]
            sn_ref[j, hh] = s_new
            o_scr[j:j + 1, hs] = jnp.sum(qcol * s_new, axis=0, keepdims=True)
    y_b = _head_rmsnorm_gate(o_scr[...], u[:, 5 * W_A:6 * W_A], p_ref[_P_HG_G:_P_HG_G + 1, :])
    y_ref[:, 0:W_A] = y_a
    y_ref[:, W_A:] = y_b


def _mix_sample(u, h0, c0, s0, pvec, wg, layer, *, bs=8):
    n = u.shape[0]
    const = lambda i: (0, 0)
    return pl.pallas_call(
        _mix_sample_kernel,
        grid=(n // bs,),
        in_specs=[
            pl.BlockSpec((bs, D_IN), lambda i: (i, 0)),
            pl.BlockSpec((None, bs, W_A), lambda i: (layer, i, 0)),
            pl.BlockSpec((None, bs, (CONV_W - 1) * W_A), lambda i: (layer, i, 0)),
            pl.BlockSpec((None, bs, HG_HEADS, HG_D, HG_D), lambda i: (layer, i, 0, 0, 0)),
            pl.BlockSpec((_P_ROWS, W_A), const),
            pl.BlockSpec((2, W_A // 2, W_A), lambda i: (0, 0, 0)),
        ],
        out_specs=[
            pl.BlockSpec((bs, D_MODEL), lambda i: (i, 0)),
            pl.BlockSpec((bs, W_A), lambda i: (i, 0)),
            pl.BlockSpec((bs, (CONV_W - 1) * W_A), lambda i: (i, 0)),
            pl.BlockSpec((bs, HG_HEADS, HG_D, HG_D), lambda i: (i, 0, 0, 0)),
        ],
        out_shape=[
            jax.ShapeDtypeStruct((n, D_MODEL), F32),
            jax.ShapeDtypeStruct((n, W_A), F32),
            jax.ShapeDtypeStruct((n, (CONV_W - 1) * W_A), F32),
            jax.ShapeDtypeStruct((n, HG_HEADS, HG_D, HG_D), F32),
        ],
        scratch_shapes=[pltpu.VMEM((bs, W_B), F32)],
        compiler_params=pltpu.CompilerParams(
            dimension_semantics=("arbitrary",), vmem_limit_bytes=VMEM_LIMIT),
        name="mix_sample",
    )(u, h0, c0, s0, pvec, wg)


def _out_sample_kernel(x_ref, y_ref, gt_ref, wout_ref, xo_ref):
    xo_ref[...] = x_ref[...] + gt_ref[...] * _bdot(y_ref[...], wout_ref[...])


def _out_sample(x, y, mod, w_out, layer):
    n = x.shape[0]
    return pl.pallas_call(
        _out_sample_kernel,
        grid=(1,),
        in_specs=[
            pl.BlockSpec((n, D_MODEL), lambda i: (0, 0)),
            pl.BlockSpec((n, D_MODEL), lambda i: (0, 0)),
            pl.BlockSpec((n, D_MODEL), lambda i: (0, 2)),
            pl.BlockSpec((None, D_MODEL, D_MODEL), lambda i: (layer, 0, 0)),
        ],
        out_specs=pl.BlockSpec((n, D_MODEL), lambda i: (0, 0)),
        out_shape=jax.ShapeDtypeStruct((n, D_MODEL), F32),
        compiler_params=pltpu.CompilerParams(
            dimension_semantics=("arbitrary",), vmem_limit_bytes=VMEM_LIMIT),
        name="out_sample",
    )(x, y, mod, w_out)


def _top2(h, router):
    h_hi = h.astype(BF16)
    h_lo = (h - h_hi.astype(F32)).astype(BF16)
    r_hi = router.astype(BF16)
    r_lo = (router - r_hi.astype(F32)).astype(BF16)
    logits = (jnp.dot(h_hi, r_hi, preferred_element_type=F32) + jnp.dot(h_hi, r_lo, preferred_element_type=F32)
              ) + jnp.dot(h_lo, r_hi, preferred_element_type=F32)
    lane = lax.broadcasted_iota(jnp.int32, logits.shape, 1).astype(F32)
    neg = -jnp.inf
    lg = jnp.where(lane < N_EXPERTS, logits, neg)
    m1 = jnp.max(lg, axis=1, keepdims=True)
    i1 = jnp.min(jnp.where(lg == m1, lane, float(LANES)), axis=1, keepdims=True)
    lg2 = jnp.where(lane == i1, neg, lg)
    m2 = jnp.max(lg2, axis=1, keepdims=True)
    i2 = jnp.min(jnp.where(lg2 == m2, lane, float(LANES)), axis=1, keepdims=True)
    e2 = jnp.exp(m2 - m1)
    den = 1.0 + e2
    return lane, i1, i2, 1.0 / den, e2 / den


def _top2_gates(h, router):
    lane, i1, i2, w1, w2 = _top2(h, router)
    return jnp.where(lane == i1, w1, 0.0) + jnp.where(lane == i2, w2, 0.0)


def _ffn_kernel(*refs, moe):
    if moe:
        (x_ref, g_ref, sh_ref, sc_ref, gt_ref, r_ref, w1_ref, w3_ref, w2_ref,
         xo_ref, h_scr, acc_scr, gate_scr) = refs
    else:
        (x_ref, g_ref, sh_ref, sc_ref, gt_ref, w1_ref, w3_ref, w2_ref,
         xo_ref, h_scr, acc_scr) = refs
    e = pl.program_id(1)
    j = pl.program_id(2)

    @pl.when((e == 0) & (j == 0))
    def _():
        h = _rmsnorm(x_ref[...], g_ref[...]) * (1.0 + sc_ref[...]) + sh_ref[...]
        h_scr[...] = h.astype(BF16)
        acc_scr[...] = jnp.zeros_like(acc_scr)
        if moe:
            gate_scr[...] = _top2_gates(h, r_ref[...])

    h = h_scr[...]
    a = jnp.dot(h, w1_ref[...].astype(BF16), preferred_element_type=F32)
    b = jnp.dot(h, w3_ref[...].astype(BF16), preferred_element_type=F32)
    part = _bdot(_silu(a) * b, w2_ref[...])
    if moe:
        lane = lax.broadcasted_iota(jnp.int32, gate_scr.shape, 1)
        part = part * jnp.sum(jnp.where(lane == e, gate_scr[...], 0.0), axis=1, keepdims=True)
    acc_scr[...] += part

    @pl.when((e == pl.num_programs(1) - 1) & (j == pl.num_programs(2) - 1))
    def _():
        xo_ref[...] = x_ref[...] + gt_ref[...] * acc_scr[...]


def _ffn(x, g2, mod, mod_spec, w1, w3, w2, wl, router=None, *, tm, tf=1408):
    n = x.shape[0]
    n_exp = w1.shape[1]
    moe = router is not None
    in_specs = [
        pl.BlockSpec((tm, D_MODEL), lambda i, e, j: (i, 0)),
        pl.BlockSpec((1, D_MODEL), lambda i, e, j: (0, 0)),
        mod_spec(3), mod_spec(4), mod_spec(5),
    ]
    args = [x, g2, mod, mod, mod]
    scratch = [pltpu.VMEM((tm, D_MODEL), BF16), pltpu.VMEM((tm, D_MODEL), F32)]
    if moe:
        in_specs.append(pl.BlockSpec((D_MODEL, LANES), lambda i, e, j: (0, 0)))
        args.append(router)
        scratch.append(pltpu.VMEM((tm, LANES), F32))
    in_specs += [
        pl.BlockSpec((None, None, D_MODEL, tf), lambda i, e, j: (wl, e, 0, j)),
        pl.BlockSpec((None, None, D_MODEL, tf), lambda i, e, j: (wl, e, 0, j)),
        pl.BlockSpec((None, None, tf, D_MODEL), lambda i, e, j: (wl, e, j, 0)),
    ]
    args += [w1, w3, w2]
    return pl.pallas_call(
        functools.partial(_ffn_kernel, moe=moe),
        grid=(n // tm, n_exp, D_FF // tf),
        in_specs=in_specs,
        out_specs=pl.BlockSpec((tm, D_MODEL), lambda i, e, j: (i, 0)),
        out_shape=jax.ShapeDtypeStruct((n, D_MODEL), F32),
        scratch_shapes=scratch,
        compiler_params=pltpu.CompilerParams(
            dimension_semantics=("arbitrary", "arbitrary", "arbitrary"), vmem_limit_bytes=VMEM_LIMIT),
        name="ffn_moe" if moe else "ffn_dense",
    )(*args)


_R_I1, _R_I2, _R_W1, _R_W2, _R_RANK1, _R_RANK2 = 0, 1, 2, 3, 4, 5
_ROW_DMA_UNROLL = 8


def _route_kernel(x_ref, g_ref, sh_ref, sc_ref, r_ref, h_ref, route_ref, cnt_ref, run_scr):
    i = pl.program_id(0)
    tm = x_ref.shape[0]

    @pl.when(i == 0)
    def _():
        run_scr[...] = jnp.zeros_like(run_scr)

    h = _rmsnorm(x_ref[...], g_ref[...]) * (1.0 + sc_ref[...]) + sh_ref[...]
    h_ref[...] = h
    lane, i_hi, i_lo, w_hi, w_lo = _top2(h, r_ref[...])
    cnt = ((lane == i_hi) | (lane == i_lo)).astype(BF16)
    rr = lax.broadcasted_iota(jnp.int32, (tm, tm), 0)
    cc = lax.broadcasted_iota(jnp.int32, (tm, tm), 1)
    before = jnp.dot((rr > cc).astype(BF16), cnt, preferred_element_type=F32) + run_scr[...]
    rank_hi = jnp.sum(jnp.where(lane == i_hi, before, 0.0), axis=1, keepdims=True)
    rank_lo = jnp.sum(jnp.where(lane == i_lo, before, 0.0), axis=1, keepdims=True)
    run_scr[...] += jnp.sum(cnt.astype(F32), axis=0, keepdims=True)
    cnt_ref[...] = run_scr[...]
    rec = jnp.zeros(lane.shape, F32)
    for ln, val in ((_R_I1, i_hi), (_R_I2, i_lo), (_R_W1, w_hi), (_R_W2, w_lo),
                    (_R_RANK1, rank_hi), (_R_RANK2, rank_lo)):
        rec = jnp.where(lane == float(ln), val, rec)
    route_ref[...] = rec


def _route(x, g2, mod, mod_spec, router, *, tm):
    n = x.shape[0]
    return pl.pallas_call(
        _route_kernel,
        grid=(n // tm,),
        in_specs=[
            pl.BlockSpec((tm, D_MODEL), lambda i: (i, 0)),
            pl.BlockSpec((1, D_MODEL), lambda i: (0, 0)),
            mod_spec(3), mod_spec(4),
            pl.BlockSpec((D_MODEL, LANES), lambda i: (0, 0)),
        ],
        out_specs=[
            pl.BlockSpec((tm, D_MODEL), lambda i: (i, 0)),
            pl.BlockSpec((tm, LANES), lambda i: (i, 0)),
            pl.BlockSpec((1, LANES), lambda i: (0, 0)),
        ],
        out_shape=[
            jax.ShapeDtypeStruct((n, D_MODEL), F32),
            jax.ShapeDtypeStruct((n, LANES), F32),
            jax.ShapeDtypeStruct((1, LANES), F32),
        ],
        scratch_shapes=[pltpu.VMEM((1, LANES), F32)],
        compiler_params=pltpu.CompilerParams(
            dimension_semantics=("arbitrary",), vmem_limit_bytes=VMEM_LIMIT),
        name="moe_route",
    )(x, g2, mod, mod, router)


def _dispatch_kernel(pos_ref, h_ref, xs_in_ref, xs_ref, sem):
    del xs_in_ref
    rows = h_ref.shape[0]

    def row_copy(r, k):
        return pltpu.make_async_copy(
            h_ref.at[pl.ds(r, 1), :], xs_ref.at[pl.ds(pos_ref[2 * r + k], 1), :], sem)

    def start(r, carry):
        row_copy(r, 0).start(priority=0)
        row_copy(r, 1).start(priority=1)
        return carry

    def wait(r, carry):
        row_copy(r, 0).wait()
        row_copy(r, 1).wait()
        return carry

    lax.fori_loop(0, rows, start, 0, unroll=_ROW_DMA_UNROLL)
    lax.fori_loop(0, rows, wait, 0, unroll=_ROW_DMA_UNROLL)


def _dispatch(h, pos, n_rows, *, tm):
    n = h.shape[0]
    xs0 = jnp.zeros((n_rows, D_MODEL), F32)
    return pl.pallas_call(
        _dispatch_kernel,
        grid=(n // tm,),
        in_specs=[
            pl.BlockSpec((2 * tm,), lambda i: (i,), memory_space=pltpu.SMEM),
            pl.BlockSpec((tm, D_MODEL), lambda i: (i, 0)),
            pl.BlockSpec(memory_space=pl.ANY),
        ],
        out_specs=pl.BlockSpec(memory_space=pl.ANY),
        out_shape=jax.ShapeDtypeStruct((n_rows, D_MODEL), F32),
        scratch_shapes=[pltpu.SemaphoreType.DMA(())],
        input_output_aliases={2: 0},
        compiler_params=pltpu.CompilerParams(
            dimension_semantics=("arbitrary",), vmem_limit_bytes=VMEM_LIMIT),
        name="moe_dispatch",
    )(pos, h, xs0)


def _gffn_kernel(te_ref, nv_ref, xs_ref, w1_ref, w3_ref, w2_ref, ys_ref):
    del te_ref
    i = pl.program_id(0)
    j = pl.program_id(1)

    @pl.when(i < nv_ref[0])
    def _():
        @pl.when(j == 0)
        def _():
            ys_ref[...] = jnp.zeros_like(ys_ref)

        h = xs_ref[...].astype(BF16)
        a = jnp.dot(h, w1_ref[...].astype(BF16), preferred_element_type=F32)
        b = jnp.dot(h, w3_ref[...].astype(BF16), preferred_element_type=F32)
        ys_ref[...] += _bdot(_silu(a) * b, w2_ref[...])

    @pl.when((i >= nv_ref[0]) & (j == 0))
    def _():
        ys_ref[...] = jnp.zeros_like(ys_ref)


def _gffn(xs, tile_expert, n_valid, w1, w3, w2, wl, *, tm, tf=1408):
    n_rows = xs.shape[0]
    last_j = D_FF // tf - 1

    def row_idx(i, j, te, nv):
        return (jnp.minimum(i, nv[0] - 1), 0)

    def ff_idx(i, j, nv):
        return jnp.where(i < nv[0], j, last_j)

    grid_spec = pltpu.PrefetchScalarGridSpec(
        num_scalar_prefetch=2,
        grid=(n_rows // tm, D_FF // tf),
        in_specs=[
            pl.BlockSpec((tm, D_MODEL), row_idx),
            pl.BlockSpec((None, None, D_MODEL, tf), lambda i, j, te, nv: (wl, te[i], 0, ff_idx(i, j, nv))),
            pl.BlockSpec((None, None, D_MODEL, tf), lambda i, j, te, nv: (wl, te[i], 0, ff_idx(i, j, nv))),
            pl.BlockSpec((None, None, tf, D_MODEL), lambda i, j, te, nv: (wl, te[i], ff_idx(i, j, nv), 0)),
        ],
        out_specs=pl.BlockSpec((tm, D_MODEL), lambda i, j, te, nv: (i, 0)),
    )
    return pl.pallas_call(
        _gffn_kernel,
        grid_spec=grid_spec,
        out_shape=jax.ShapeDtypeStruct((n_rows, D_MODEL), F32),
        compiler_params=pltpu.CompilerParams(
            dimension_semantics=("arbitrary", "arbitrary"), vmem_limit_bytes=VMEM_LIMIT),
        name="moe_grouped_ffn",
    )(tile_expert, n_valid, xs, w1, w3, w2)


def _combine_kernel(pos_ref, x_ref, gt_ref, route_ref, fg_ref, ys_ref, xo_ref, y0_scr, y1_scr, sem, *, final_norm):
    rows = x_ref.shape[0]

    def row_copy(r, k):
        buf = y0_scr if k == 0 else y1_scr
        return pltpu.make_async_copy(
            ys_ref.at[pl.ds(pos_ref[2 * r + k], 1), :], buf.at[pl.ds(r, 1), :], sem)

    def start(r, carry):
        row_copy(r, 0).start(priority=0)
        row_copy(r, 1).start(priority=1)
        return carry

    def wait(r, carry):
        row_copy(r, 0).wait()
        row_copy(r, 1).wait()
        return carry

    lax.fori_loop(0, rows, start, 0, unroll=_ROW_DMA_UNROLL)
    lax.fori_loop(0, rows, wait, 0, unroll=_ROW_DMA_UNROLL)
    route = route_ref[...]
    fo = route[:, _R_W1:_R_W1 + 1] * y0_scr[...] + route[:, _R_W2:_R_W2 + 1] * y1_scr[...]
    x_new = x_ref[...] + gt_ref[...] * fo
    xo_ref[...] = _rmsnorm(x_new, fg_ref[...]) if final_norm else x_new


def _combine(x, mod, mod_spec, route, pos, ys, final_g, *, tm):
    n = x.shape[0]
    final_norm = final_g is not None
    fg = final_g if final_norm else jnp.ones((1, D_MODEL), F32)
    return pl.pallas_call(
        functools.partial(_combine_kernel, final_norm=final_norm),
        grid=(n // tm,),
        in_specs=[
            pl.BlockSpec((2 * tm,), lambda i: (i,), memory_space=pltpu.SMEM),
            pl.BlockSpec((tm, D_MODEL), lambda i: (i, 0)),
            mod_spec(5),
            pl.BlockSpec((tm, LANES), lambda i: (i, 0)),
            pl.BlockSpec((1, D_MODEL), lambda i: (0, 0)),
            pl.BlockSpec(memory_space=pl.ANY),
        ],
        out_specs=pl.BlockSpec((tm, D_MODEL), lambda i: (i, 0)),
        out_shape=jax.ShapeDtypeStruct((n, D_MODEL), F32),
        scratch_shapes=[pltpu.VMEM((tm, D_MODEL), F32), pltpu.VMEM((tm, D_MODEL), F32),
                        pltpu.SemaphoreType.DMA(())],
        compiler_params=pltpu.CompilerParams(
            dimension_semantics=("arbitrary",), vmem_limit_bytes=VMEM_LIMIT),
        name="moe_combine",
    )(pos, x, mod, route, fg, ys)


def _moe_routed(x, g2, mod, tiles_per_seq_of, router, w1, w3, w2, wl, final_g=None, *,
                tm_route=512, tm_move=512, tm_group=512):
    n = x.shape[0]

    def spec1(tm):
        return lambda c: pl.BlockSpec((None, 1, D_MODEL), lambda i: (i // tiles_per_seq_of(tm), 0, c))

    h, route, counts = _route(x, g2, mod, spec1(tm_route), router, tm=tm_route)
    cnt = counts[0, :N_EXPERTS].astype(jnp.int32)
    tiles_e = (cnt + tm_group - 1) // tm_group
    tile_start = jnp.cumsum(tiles_e) - tiles_e
    n_valid = jnp.sum(tiles_e)
    n_tiles = (2 * n) // tm_group + N_EXPERTS
    tile_ids = jnp.minimum(jnp.arange(n_tiles, dtype=jnp.int32), n_valid - 1)
    tile_expert = (jnp.sum(tile_ids[:, None] >= tile_start[None, :], axis=1) - 1).astype(jnp.int32)
    idx = route[:, _R_I1:_R_I2 + 1].astype(jnp.int32)
    rank = route[:, _R_RANK1:_R_RANK2 + 1].astype(jnp.int32)
    pos = ((tile_start * tm_group)[idx] + rank).reshape(2 * n)

    xs = _dispatch(h, pos, n_tiles * tm_group, tm=tm_move)
    ys = _gffn(xs, tile_expert, n_valid.reshape(1).astype(jnp.int32), w1, w3, w2, wl, tm=tm_group)
    return _combine(x, mod, spec1(tm_move), route, pos, ys, final_g, tm=tm_move)


def _final_norm_kernel(x_ref, g_ref, o_ref):
    o_ref[...] = _rmsnorm(x_ref[...], g_ref[...])


def _final_norm(x, g, *, tm):
    n = x.shape[0]
    return pl.pallas_call(
        _final_norm_kernel,
        grid=(n // tm,),
        in_specs=[pl.BlockSpec((tm, D_MODEL), lambda i: (i, 0)), pl.BlockSpec((1, D_MODEL), lambda i: (0, 0))],
        out_specs=pl.BlockSpec((tm, D_MODEL), lambda i: (i, 0)),
        out_shape=jax.ShapeDtypeStruct((n, D_MODEL), F32),
        compiler_params=pltpu.CompilerParams(
            dimension_semantics=("arbitrary",), vmem_limit_bytes=VMEM_LIMIT),
        name="final_norm",
    )(x, g)


def _block_diag(w):
    nb, bw, _ = w.shape
    eye = jnp.eye(nb, dtype=w.dtype)
    return (eye[:, None, :, None] * w[:, :, None, :]).reshape(nb * bw, nb * bw)


def kernel(x_prompt, x_sample, c_prompt, c_sample, state_rglru_h, state_rglru_conv, state_hgrn,
           mod_w, mod_b, norm1_g, norm2_g, w_in, conv_w, conv_b, rg_wa, rg_ba, rg_wx, rg_bx,
           rg_lambda, rg_out_g, hg_lb_logits, hg_norm_g, w_out, ffn_w1, ffn_w3, ffn_w2,
           router_w, moe_w1, moe_w3, moe_w2, final_g):
    n_b, seq, _ = x_prompt.shape
    n_s = x_sample.shape[0]
    tm_p = 512
    tiles_per_seq = seq // tm_p

    mod = _modulation(jnp.concatenate([c_prompt, c_sample], axis=0), mod_w, mod_b)
    lbs = _lower_bounds(hg_lb_logits)

    def row(v):
        return v.reshape(1, -1)

    w_in_b = w_in.astype(BF16)
    w_out_b = w_out.astype(BF16)
    dense_w = [w.astype(BF16)[:, None] for w in (ffn_w1, ffn_w3, ffn_w2)]
    moe_w = [w.astype(BF16) for w in (moe_w1, moe_w3, moe_w2)]
    conv_all = state_rglru_conv.reshape(DEPTH, n_s, (CONV_W - 1) * W_A)

    xp = x_prompt
    xs = x_sample.reshape(n_s, D_MODEL)
    outs = {k: [] for k in ("hp", "cp", "sp", "hs", "cs", "ss")}
    for l in range(DEPTH):
        mod_p = mod[l, :n_b].reshape(n_b, 1, 6 * D_MODEL)
        mod_s = mod[l, n_b:]
        pvec = jnp.concatenate([
            conv_w[l], row(conv_b[l]), row(rg_ba[l]), row(rg_bx[l]), row(rg_lambda[l]), row(rg_out_g[l]),
            row(lbs[l]), row(hg_norm_g[l]), jnp.zeros((_P_ROWS - 11, W_A), F32)], axis=0)
        bd_a, bd_x, half = _block_diag(rg_wa[l]), _block_diag(rg_wx[l]), W_A // 2
        wg = jnp.stack([
            jnp.concatenate([bd[hh * half:(hh + 1) * half, hh * half:(hh + 1) * half] for bd in (bd_a, bd_x)], axis=1)
            for hh in range(2)]).astype(BF16)

        xp, hp, tail, sp = _mix_prompt(xp, row(norm1_g[l]), mod_p, w_in_b, pvec, wg, w_out_b, l)
        outs["hp"].append(hp.reshape(n_b, W_A))
        outs["cp"].append(tail[:, SUBLANES - (CONV_W - 1):, :])
        outs["sp"].append(sp)

        u_s = _in_sample(xs, row(norm1_g[l]), mod_s, w_in_b, l)
        y_s, hs, cs, ss = _mix_sample(u_s, state_rglru_h, conv_all, state_hgrn, pvec, wg, l)
        xs = _out_sample(xs, y_s, mod_s, w_out_b, l)
        outs["hs"].append(hs)
        outs["cs"].append(cs.reshape(n_s, CONV_W - 1, W_A))
        outs["ss"].append(ss)

        if l % 2 == 0:
            w1, w3, w2 = dense_w
            router = None
        else:
            w1, w3, w2 = moe_w
            router = jnp.pad(router_w[l // 2], ((0, 0), (0, LANES - N_EXPERTS)))
        spec_p = lambda c: pl.BlockSpec((None, 1, D_MODEL), lambda i, e, j: (i // tiles_per_seq, 0, c))
        spec_s = lambda c: pl.BlockSpec((n_s, D_MODEL), lambda i, e, j: (0, c))
        if router is None:
            xp = _ffn(xp.reshape(n_b * seq, D_MODEL), row(norm2_g[l]), mod_p, spec_p, w1, w3, w2, l // 2, None,
                      tm=tm_p)
        else:
            xp = _moe_routed(xp.reshape(n_b * seq, D_MODEL), row(norm2_g[l]), mod_p, lambda tm: seq // tm,
                             router, w1, w3, w2, l // 2, row(final_g) if l == DEPTH - 1 else None)
        xp = xp.reshape(n_b, seq, D_MODEL)
        xs = _ffn(xs, row(norm2_g[l]), mod_s, spec_s, w1, w3, w2, l // 2, router, tm=n_s)

    y_prompt = xp
    y_sample = _final_norm(xs, row(final_g), tm=n_s).reshape(n_s, 1, D_MODEL)
    return (y_prompt, y_sample,
            jnp.stack(outs["hp"]), jnp.stack(outs["cp"]), jnp.stack(outs["sp"]),
            jnp.stack(outs["hs"]), jnp.stack(outs["cs"]), jnp.stack(outs["ss"]))
```

```python
import functools

import jax
import jax.numpy as jnp
from jax import lax
from jax.experimental import pallas as pl
from jax.experimental.pallas import tpu as pltpu

F32 = jnp.float32
BF16 = jnp.bfloat16
HIGHEST = lax.Precision.HIGHEST

D_MODEL = 1024
DEPTH = 4
W_A = 512
RG_BLOCKS = 8
CONV_W = 4
RG_C = 8.0
W_B = 512
HG_HEADS = 4
HG_D = 128
D_IN = 3072
D_FF = 2816
N_EXPERTS = 8
EPS = 1e-6
PAST_LEN = 16384

LANES = 128
SUBLANES = 8
VMEM_LIMIT = 56 * 1024 * 1024

_P_CONV_W, _P_CONV_B, _P_BA, _P_BX, _P_LAM, _P_RG_G, _P_LB, _P_HG_G = 0, 4, 5, 6, 7, 8, 9, 10
_P_ROWS = 16


def _silu(x):
    return x * jax.nn.sigmoid(x)


def _gelu_tanh(x):
    cdf = 0.5 * (1.0 + jnp.tanh(0.7978845608028654 * (x + 0.044715 * (x * x * x))))
    return x * cdf


def _rmsnorm(x, g):
    return x * lax.rsqrt(jnp.mean(x * x, axis=-1, keepdims=True) + EPS) * g


def _softplus(z):
    return jnp.maximum(z, 0.0) + jnp.log1p(jnp.exp(-jnp.abs(z)))


def _bdot(a, b):
    return jnp.dot(a.astype(BF16), b.astype(BF16), preferred_element_type=F32)


def _rglru_coeffs(xc, wg, p_ref):
    half = W_A // 2
    g0 = _bdot(xc[:, :half], wg[0])
    g1 = _bdot(xc[:, half:], wg[1])
    r = jax.nn.sigmoid(jnp.concatenate([g0[:, :half], g1[:, :half]], axis=1) + p_ref[_P_BA:_P_BA + 1, :])
    i = jax.nn.sigmoid(jnp.concatenate([g0[:, half:], g1[:, half:]], axis=1) + p_ref[_P_BX:_P_BX + 1, :])
    log_a = (-RG_C * _softplus(-p_ref[_P_LAM:_P_LAM + 1, :])) * r
    a = jnp.exp(log_a)
    one_minus_a2 = -jnp.tanh(log_a) * (a * a + 1.0)
    return a, one_minus_a2, i * xc


def _sqrt_nonneg(x):
    return jnp.where(x == 0.0, 0.0, x * lax.rsqrt(x))


def _group_roll(x, k):
    r, w = x.shape
    return pltpu.roll(x.reshape(r // SUBLANES, SUBLANES, w), k, 1).reshape(r, w)


def _scan_rows(a, bb, h0):
    r, w = a.shape
    sub = lax.broadcasted_iota(jnp.int32, (r, w), 0) & (SUBLANES - 1)
    k = 1
    while k < SUBLANES:
        keep = sub >= k
        bb = a * jnp.where(keep, _group_roll(bb, k), 0.0) + bb
        a = a * jnp.where(keep, _group_roll(a, k), 1.0)
        k *= 2
    outs = []
    carry = h0
    for g in range(r // SUBLANES):
        rows = slice(g * SUBLANES, (g + 1) * SUBLANES)
        hg = a[rows] * carry + bb[rows]
        outs.append(hg)
        carry = hg[SUBLANES - 1:SUBLANES, :]
    return jnp.concatenate(outs, axis=0)


def _head_rmsnorm_gate(o, gb, g_row):
    outs = []
    for h in range(HG_HEADS):
        hs = slice(h * HG_D, (h + 1) * HG_D)
        outs.append(_rmsnorm(o[:, hs], g_row[:, hs]))
    return jnp.concatenate(outs, axis=1) * _silu(gb)


def _mod_kernel(c_ref, w_ref, b_ref, o_ref):
    o_ref[...] = _bdot(_silu(c_ref[...]), w_ref[...]) + b_ref[...]


def _modulation(c_all, mod_w, mod_b):
    rows = c_all.shape[0]
    tn = 1536
    n_out = mod_w.shape[-1]
    return pl.pallas_call(
        _mod_kernel,
        grid=(DEPTH, n_out // tn),
        in_specs=[
            pl.BlockSpec((rows, D_MODEL), lambda l, j: (0, 0)),
            pl.BlockSpec((None, D_MODEL, tn), lambda l, j: (l, 0, j)),
            pl.BlockSpec((None, 1, tn), lambda l, j: (l, 0, j)),
        ],
        out_specs=pl.BlockSpec((None, rows, tn), lambda l, j: (l, 0, j)),
        out_shape=jax.ShapeDtypeStruct((DEPTH, rows, n_out), F32),
        compiler_params=pltpu.CompilerParams(
            dimension_semantics=("arbitrary", "arbitrary"), vmem_limit_bytes=VMEM_LIMIT),
        name="adaln_mod",
    )(c_all, mod_w, mod_b.reshape(DEPTH, 1, n_out))


def _lb_kernel(l_ref, o_ref):
    x = l_ref[...]
    e = jnp.exp(x - jnp.max(x, axis=0, keepdims=True))
    p = e / jnp.sum(e, axis=0, keepdims=True)
    p0 = p[0:1, :]
    run = p0
    o_ref[0:1, :] = run - p0
    for r in range(1, DEPTH):
        run = run + p[r:r + 1, :]
        o_ref[r:r + 1, :] = run - p0


def _lower_bounds(hg_lb_logits):
    return pl.pallas_call(
        _lb_kernel,
        out_shape=jax.ShapeDtypeStruct(hg_lb_logits.shape, F32),
        name="hgrn_lower_bounds",
    )(hg_lb_logits)


def _hgrn_chunk(q, k, v, f, logf, st):
    C = q.shape[0]
    G = C // SUBLANES
    rc = lax.broadcasted_iota(jnp.int32, (C, C), 0)
    cc = lax.broadcasted_iota(jnp.int32, (C, C), 1)
    sr = lax.broadcasted_iota(jnp.int32, (C + 2 * G, C), 0)
    sc = lax.broadcasted_iota(jnp.int32, (C + 2 * G, C), 1)
    bound = jnp.where(sr < C, sr + 1,
                      jnp.where(sr < C + G, (sr - C) * SUBLANES, (sr - C - G + 1) * SUBLANES))
    hi = logf.astype(BF16)
    rest = logf - hi.astype(F32)
    mid = rest.astype(BF16)
    lo = (rest - mid.astype(F32)).astype(BF16)
    sums = jnp.dot((sc < bound).astype(BF16), jnp.concatenate([hi, mid, lo], axis=1), preferred_element_type=F32)
    sums = (sums[:, :W_B] + sums[:, W_B:2 * W_B]) + sums[:, 2 * W_B:]
    b, g_start, g_end = sums[:C], sums[C:C + G], sums[C + G:]
    row = lax.broadcasted_iota(jnp.int32, (C, W_B), 0)
    xor = rc ^ cc

    heads = [slice(h * HG_D, (h + 1) * HG_D) for h in range(HG_HEADS)]
    nt = (((1,), (1,)), ((), ()))
    qb, kb = q.astype(BF16), k.astype(BF16)
    att = [jnp.where(rc == cc, lax.dot_general(qb[:, hs], kb[:, hs], nt, preferred_element_type=F32), 0.0)
           for hs in heads]

    def add_level(x, m, lg):
        mask = ((xor >> lg) == 1) & ((rc & m) != 0)
        for h, hs in enumerate(heads):
            p = lax.dot_general(x[:, hs], x[:, hs], nt, preferred_element_type=F32)
            att[h] = jnp.where(mask, p, att[h])

    c = b - logf
    d = b
    m, lg = 1, 0
    while m < SUBLANES:
        odd = (row & m) != 0
        if m == 1:
            x = jnp.where(odd, q * f, k)
        else:
            x = jnp.where(odd, q, k) * jnp.exp(jnp.where(odd, b - c, d - b))
        add_level(x.astype(BF16), m, lg)
        c = jnp.where(odd, _group_roll(c, m), c)
        d = jnp.where(odd, d, _group_roll(d, SUBLANES - m))
        m, lg = 2 * m, lg + 1

    q8 = q * jnp.exp(b - c)
    k8 = k * jnp.exp(d - b)
    grow = lax.broadcasted_iota(jnp.int32, (G, W_B), 0)
    cg, dg = g_start, g_end
    mu = 1
    while mu < G:
        oddg = (grow & mu) != 0
        rg = jnp.exp(jnp.where(oddg, g_start - cg, dg - g_end))
        pieces = []
        for g in range(G):
            base = q8 if (g // mu) % 2 == 1 else k8
            pieces.append(base[g * SUBLANES:(g + 1) * SUBLANES] * rg[g:g + 1, :])
        add_level(jnp.concatenate(pieces, axis=0).astype(BF16), m, lg)
        cg = jnp.where(oddg, pltpu.roll(cg, mu, 0), cg)
        dg = jnp.where(oddg, dg, pltpu.roll(dg, G - mu, 0))
        mu, m, lg = 2 * mu, 2 * m, lg + 1

    b_last = b[C - 1:C, :]
    qe = (q * jnp.exp(b)).astype(BF16)
    kd = (k * jnp.exp(b_last - b)).astype(BF16)
    vb = v.astype(BF16)
    tn = (((0,), (0,)), ((), ()))
    outs, new_st = [], []
    for h, hs in enumerate(heads):
        o_inter = lax.dot_general(qe[:, hs], st[h].astype(BF16), nt, preferred_element_type=F32)
        o_intra = jnp.dot(att[h].astype(BF16), vb[:, hs], preferred_element_type=F32)
        outs.append(o_inter + o_intra)
        upd = lax.dot_general(vb[:, hs], kd[:, hs], tn, preferred_element_type=F32)
        new_st.append(jnp.exp(b_last[:, hs]) * st[h] + upd)
    return jnp.concatenate(outs, axis=1), new_st


def _mix_prompt_kernel(x_ref, g1_ref, sh_ref, sc_ref, gt_ref, win_ref, p_ref, wg_ref, wout_ref,
                       xo_ref, hn_ref, tail_ref, sn_ref,
                       hcar, ext, st_scr, *, hg_chunk):
    t = pl.program_id(1)
    tc = x_ref.shape[0]

    @pl.when(t == 0)
    def _():
        hcar[...] = jnp.zeros_like(hcar)
        ext[0:SUBLANES, :] = jnp.zeros((SUBLANES, W_A), F32)
        st_scr[...] = jnp.zeros_like(st_scr)

    x = x_ref[...]
    h = _rmsnorm(x, g1_ref[...]) * (1.0 + sc_ref[...]) + sh_ref[...]
    u = _bdot(h, win_ref[...])

    xa = u[:, 0:W_A]
    ga = u[:, W_A:2 * W_A]
    ext[SUBLANES:, :] = xa
    xc = p_ref[_P_CONV_B:_P_CONV_B + 1, :]
    for j in range(CONV_W):
        lo = SUBLANES - (CONV_W - 1) + j
        xc = xc + ext[lo:lo + tc, :] * p_ref[_P_CONV_W + j:_P_CONV_W + j + 1, :]
    tail = xa[tc - SUBLANES:, :]
    ext[0:SUBLANES, :] = tail
    tail_ref[...] = tail

    a, one_minus_a2, ix = _rglru_coeffs(xc, wg_ref, p_ref)
    row = lax.broadcasted_iota(jnp.int32, (tc, W_A), 0)
    first = (row == 0) & (t == 0)
    a = jnp.where(first, 0.0, a)
    bb = jnp.where(first, 1.0, _sqrt_nonneg(one_minus_a2)) * ix
    hseq = _scan_rows(a, bb, hcar[...])
    h_last = hseq[tc - 1:tc, :]
    hcar[...] = h_last
    hn_ref[...] = h_last
    y_a = _rmsnorm(hseq * _gelu_tanh(ga), p_ref[_P_RG_G:_P_RG_G + 1, :])

    lb = p_ref[_P_LB:_P_LB + 1, :]
    f = lb + (1.0 - lb) * jax.nn.sigmoid(u[:, 3 * W_A:4 * W_A])
    q = _silu(u[:, 2 * W_A:3 * W_A])
    kk = 1.0 - f
    logf = jnp.log(f)
    v = u[:, 4 * W_A:5 * W_A]
    st = [st_scr[hh] for hh in range(HG_HEADS)]
    outs = []
    for c0 in range(0, tc, hg_chunk):
        cs = slice(c0, c0 + hg_chunk)
        o_c, st = _hgrn_chunk(q[cs], kk[cs], v[cs], f[cs], logf[cs], st)
        outs.append(o_c)
    o = outs[0] if len(outs) == 1 else jnp.concatenate(outs, axis=0)
    for hh in range(HG_HEADS):
        st_scr[hh] = st[hh]
        sn_ref[hh] = st[hh].T
    y_b = _head_rmsnorm_gate(o, u[:, 5 * W_A:6 * W_A], p_ref[_P_HG_G:_P_HG_G + 1, :])

    y = _bdot(jnp.concatenate([y_a, y_b], axis=1), wout_ref[...])
    xo_ref[...] = x + gt_ref[...] * y


def _mix_prompt(x, g1, mod, w_in, pvec, wg, w_out, layer, *, tc=512, hg_chunk=128):
    bsz, seq, _ = x.shape
    const = lambda b, t: (0, 0)
    wconst = lambda b, t: (layer, 0, 0)
    kern = functools.partial(_mix_prompt_kernel, hg_chunk=hg_chunk)
    return pl.pallas_call(
        kern,
        grid=(bsz, seq // tc),
        in_specs=[
            pl.BlockSpec((None, tc, D_MODEL), lambda b, t: (b, t, 0)),
            pl.BlockSpec((1, D_MODEL), const),
            pl.BlockSpec((None, 1, D_MODEL), lambda b, t: (b, 0, 0)),
            pl.BlockSpec((None, 1, D_MODEL), lambda b, t: (b, 0, 1)),
            pl.BlockSpec((None, 1, D_MODEL), lambda b, t: (b, 0, 2)),
            pl.BlockSpec((None, D_MODEL, D_IN), wconst),
            pl.BlockSpec((_P_ROWS, W_A), const),
            pl.BlockSpec((2, W_A // 2, W_A), lambda b, t: (0, 0, 0)),
            pl.BlockSpec((None, D_MODEL, D_MODEL), wconst),
        ],
        out_specs=[
            pl.BlockSpec((None, tc, D_MODEL), lambda b, t: (b, t, 0)),
            pl.BlockSpec((None, 1, W_A), lambda b, t: (b, 0, 0)),
            pl.BlockSpec((None, SUBLANES, W_A), lambda b, t: (b, 0, 0)),
            pl.BlockSpec((None, HG_HEADS, HG_D, HG_D), lambda b, t: (b, 0, 0, 0)),
        ],
        out_shape=[
            jax.ShapeDtypeStruct(x.shape, F32),
            jax.ShapeDtypeStruct((bsz, 1, W_A), F32),
            jax.ShapeDtypeStruct((bsz, SUBLANES, W_A), F32),
            jax.ShapeDtypeStruct((bsz, HG_HEADS, HG_D, HG_D), F32),
        ],
        scratch_shapes=[
            pltpu.VMEM((1, W_A), F32),
            pltpu.VMEM((tc + SUBLANES, W_A), F32),
            pltpu.VMEM((HG_HEADS, HG_D, HG_D), F32),
        ],
        compiler_params=pltpu.CompilerParams(
            dimension_semantics=("arbitrary", "arbitrary"), vmem_limit_bytes=VMEM_LIMIT),
        name="mix_prompt",
    )(x, g1, mod, mod, mod, w_in, pvec, wg, w_out)


def _in_sample_kernel(x_ref, g1_ref, sh_ref, sc_ref, win_ref, u_ref):
    h = _rmsnorm(x_ref[...], g1_ref[...]) * (1.0 + sc_ref[...]) + sh_ref[...]
    u_ref[...] = _bdot(h, win_ref[...])


def _in_sample(x, g1, mod, w_in, layer):
    n = x.shape[0]
    tn = 1024
    return pl.pallas_call(
        _in_sample_kernel,
        grid=(D_IN // tn,),
        in_specs=[
            pl.BlockSpec((n, D_MODEL), lambda j: (0, 0)),
            pl.BlockSpec((1, D_MODEL), lambda j: (0, 0)),
            pl.BlockSpec((n, D_MODEL), lambda j: (0, 0)),
            pl.BlockSpec((n, D_MODEL), lambda j: (0, 1)),
            pl.BlockSpec((None, D_MODEL, tn), lambda j: (layer, 0, j)),
        ],
        out_specs=pl.BlockSpec((n, tn), lambda j: (0, j)),
        out_shape=jax.ShapeDtypeStruct((n, D_IN), F32),
        compiler_params=pltpu.CompilerParams(
            dimension_semantics=("arbitrary",), vmem_limit_bytes=VMEM_LIMIT),
        name="in_sample",
    )(x, g1, mod, mod, w_in)


def _mix_sample_kernel(u_ref, h0_ref, c0_ref, s0_ref, p_ref, wg_ref,
                       y_ref, hn_ref, cn_ref, sn_ref, o_scr):
    bs = u_ref.shape[0]
    u = u_ref[...]
    xa = u[:, 0:W_A]
    ga = u[:, W_A:2 * W_A]
    c0 = c0_ref[...]
    xc = p_ref[_P_CONV_B:_P_CONV_B + 1, :]
    for j in range(CONV_W - 1):
        xc = xc + c0[:, j * W_A:(j + 1) * W_A] * p_ref[_P_CONV_W + j:_P_CONV_W + j + 1, :]
    xc = xc + xa * p_ref[_P_CONV_W + CONV_W - 1:_P_CONV_W + CONV_W, :]
    cn_ref[:, 0:2 * W_A] = c0[:, W_A:]
    cn_ref[:, 2 * W_A:] = xa

    a, one_minus_a2, ix = _rglru_coeffs(xc, wg_ref, p_ref)
    hnew = _sqrt_nonneg(one_minus_a2) * ix + a * h0_ref[...]
    hn_ref[...] = hnew
    y_a = _rmsnorm(hnew * _gelu_tanh(ga), p_ref[_P_RG_G:_P_RG_G + 1, :])

    lb = p_ref[_P_LB:_P_LB + 1, :]
    f = lb + (1.0 - lb) * jax.nn.sigmoid(u[:, 3 * W_A:4 * W_A])
    q = _silu(u[:, 2 * W_A:3 * W_A])
    v = u[:, 4 * W_A:5 * W_A]
    eye = (lax.broadcasted_iota(jnp.int32, (bs, bs), 0) == lax.broadcasted_iota(jnp.int32, (bs, bs), 1)).astype(F32)
    fcols = lax.dot_general(f, eye, (((0,), (0,)), ((), ())), precision=HIGHEST, preferred_element_type=F32)
    qb = q.astype(BF16)
    for j in range(bs):
        for hh in range(HG_HEADS):
            hs = slice(hh * HG_D, (hh + 1) * HG_D)
            fb = jnp.broadcast_to(fcols[hh * HG_D:(hh + 1) * HG_D, j:j + 1], (HG_D, HG_D))
            s_new = fb * s0_ref[j, hh] + (1.0 - fb) * v[j:j + 1, hs]
            sn_ref[j, hh] = s_new
            q_rows = jnp.broadcast_to(qb[j:j + 1, hs], (SUBLANES, HG_D))
            o_rows = jnp.dot(q_rows, s_new.astype(BF16), preferred_element_type=F32)
            o_scr[j:j + 1, hs] = o_rows[0:1, :]
    y_b = _head_rmsnorm_gate(o_scr[...], u[:, 5 * W_A:6 * W_A], p_ref[_P_HG_G:_P_HG_G + 1, :])
    y_ref[:, 0:W_A] = y_a
    y_ref[:, W_A:] = y_b


def _mix_sample(u, h0, c0, s0, pvec, wg, layer, *, bs=8):
    n = u.shape[0]
    const = lambda i: (0, 0)
    return pl.pallas_call(
        _mix_sample_kernel,
        grid=(n // bs,),
        in_specs=[
            pl.BlockSpec((bs, D_IN), lambda i: (i, 0)),
            pl.BlockSpec((None, bs, W_A), lambda i: (layer, i, 0)),
            pl.BlockSpec((None, bs, (CONV_W - 1) * W_A), lambda i: (layer, i, 0)),
            pl.BlockSpec((None, bs, HG_HEADS, HG_D, HG_D), lambda i: (layer, i, 0, 0, 0)),
            pl.BlockSpec((_P_ROWS, W_A), const),
            pl.BlockSpec((2, W_A // 2, W_A), lambda i: (0, 0, 0)),
        ],
        out_specs=[
            pl.BlockSpec((bs, D_MODEL), lambda i: (i, 0)),
            pl.BlockSpec((bs, W_A), lambda i: (i, 0)),
            pl.BlockSpec((bs, (CONV_W - 1) * W_A), lambda i: (i, 0)),
            pl.BlockSpec((bs, HG_HEADS, HG_D, HG_D), lambda i: (i, 0, 0, 0)),
        ],
        out_shape=[
            jax.ShapeDtypeStruct((n, D_MODEL), F32),
            jax.ShapeDtypeStruct((n, W_A), F32),
            jax.ShapeDtypeStruct((n, (CONV_W - 1) * W_A), F32),
            jax.ShapeDtypeStruct((n, HG_HEADS, HG_D, HG_D), F32),
        ],
        scratch_shapes=[pltpu.VMEM((bs, W_B), F32)],
        compiler_params=pltpu.CompilerParams(
            dimension_semantics=("arbitrary",), vmem_limit_bytes=VMEM_LIMIT),
        name="mix_sample",
    )(u, h0, c0, s0, pvec, wg)


def _out_sample_kernel(x_ref, y_ref, gt_ref, wout_ref, xo_ref):
    xo_ref[...] = x_ref[...] + gt_ref[...] * _bdot(y_ref[...], wout_ref[...])


def _out_sample(x, y, mod, w_out, layer):
    n = x.shape[0]
    return pl.pallas_call(
        _out_sample_kernel,
        grid=(1,),
        in_specs=[
            pl.BlockSpec((n, D_MODEL), lambda i: (0, 0)),
            pl.BlockSpec((n, D_MODEL), lambda i: (0, 0)),
            pl.BlockSpec((n, D_MODEL), lambda i: (0, 2)),
            pl.BlockSpec((None, D_MODEL, D_MODEL), lambda i: (layer, 0, 0)),
        ],
        out_specs=pl.BlockSpec((n, D_MODEL), lambda i: (0, 0)),
        out_shape=jax.ShapeDtypeStruct((n, D_MODEL), F32),
        compiler_params=pltpu.CompilerParams(
            dimension_semantics=("arbitrary",), vmem_limit_bytes=VMEM_LIMIT),
        name="out_sample",
    )(x, y, mod, w_out)


def _top2(h, router):
    h_hi = h.astype(BF16)
    h_lo = (h - h_hi.astype(F32)).astype(BF16)
    r_hi = router.astype(BF16)
    r_lo = (router - r_hi.astype(F32)).astype(BF16)
    logits = (jnp.dot(h_hi, r_hi, preferred_element_type=F32) + jnp.dot(h_hi, r_lo, preferred_element_type=F32)
              ) + jnp.dot(h_lo, r_hi, preferred_element_type=F32)
    lane = lax.broadcasted_iota(jnp.int32, logits.shape, 1).astype(F32)
    neg = -jnp.inf
    lg = jnp.where(lane < N_EXPERTS, logits, neg)
    m1 = jnp.max(lg, axis=1, keepdims=True)
    i1 = jnp.min(jnp.where(lg == m1, lane, float(LANES)), axis=1, keepdims=True)
    lg2 = jnp.where(lane == i1, neg, lg)
    m2 = jnp.max(lg2, axis=1, keepdims=True)
    i2 = jnp.min(jnp.where(lg2 == m2, lane, float(LANES)), axis=1, keepdims=True)
    e2 = jnp.exp(m2 - m1)
    den = 1.0 + e2
    return lane, i1, i2, 1.0 / den, e2 / den


def _top2_gates(h, router):
    lane, i1, i2, w1, w2 = _top2(h, router)
    return jnp.where(lane == i1, w1, 0.0) + jnp.where(lane == i2, w2, 0.0)


def _ffn_kernel(*refs, moe):
    if moe:
        (x_ref, g_ref, sh_ref, sc_ref, gt_ref, r_ref, w1_ref, w3_ref, w2_ref,
         xo_ref, h_scr, acc_scr, gate_scr) = refs
    else:
        (x_ref, g_ref, sh_ref, sc_ref, gt_ref, w1_ref, w3_ref, w2_ref,
         xo_ref, h_scr, acc_scr) = refs
    e = pl.program_id(1)
    j = pl.program_id(2)

    @pl.when((e == 0) & (j == 0))
    def _():
        h = _rmsnorm(x_ref[...], g_ref[...]) * (1.0 + sc_ref[...]) + sh_ref[...]
        h_scr[...] = h.astype(BF16)
        acc_scr[...] = jnp.zeros_like(acc_scr)
        if moe:
            gate_scr[...] = _top2_gates(h, r_ref[...])

    h = h_scr[...]
    a = jnp.dot(h, w1_ref[...].astype(BF16), preferred_element_type=F32)
    b = jnp.dot(h, w3_ref[...].astype(BF16), preferred_element_type=F32)
    part = _bdot(_silu(a) * b, w2_ref[...])
    if moe:
        lane = lax.broadcasted_iota(jnp.int32, gate_scr.shape, 1)
        part = part * jnp.sum(jnp.where(lane == e, gate_scr[...], 0.0), axis=1, keepdims=True)
    acc_scr[...] += part

    @pl.when((e == pl.num_programs(1) - 1) & (j == pl.num_programs(2) - 1))
    def _():
        xo_ref[...] = x_ref[...] + gt_ref[...] * acc_scr[...]


def _ffn(x, g2, mod, mod_spec, w1, w3, w2, wl, router=None, *, tm, tf=1408):
    n = x.shape[0]
    n_exp = w1.shape[1]
    moe = router is not None
    in_specs = [
        pl.BlockSpec((tm, D_MODEL), lambda i, e, j: (i, 0)),
        pl.BlockSpec((1, D_MODEL), lambda i, e, j: (0, 0)),
        mod_spec(3), mod_spec(4), mod_spec(5),
    ]
    args = [x, g2, mod, mod, mod]
    scratch = [pltpu.VMEM((tm, D_MODEL), BF16), pltpu.VMEM((tm, D_MODEL), F32)]
    if moe:
        in_specs.append(pl.BlockSpec((D_MODEL, LANES), lambda i, e, j: (0, 0)))
        args.append(router)
        scratch.append(pltpu.VMEM((tm, LANES), F32))
    in_specs += [
        pl.BlockSpec((None, None, D_MODEL, tf), lambda i, e, j: (wl, e, 0, j)),
        pl.BlockSpec((None, None, D_MODEL, tf), lambda i, e, j: (wl, e, 0, j)),
        pl.BlockSpec((None, None, tf, D_MODEL), lambda i, e, j: (wl, e, j, 0)),
    ]
    args += [w1, w3, w2]
    return pl.pallas_call(
        functools.partial(_ffn_kernel, moe=moe),
        grid=(n // tm, n_exp, D_FF // tf),
        in_specs=in_specs,
        out_specs=pl.BlockSpec((tm, D_MODEL), lambda i, e, j: (i, 0)),
        out_shape=jax.ShapeDtypeStruct((n, D_MODEL), F32),
        scratch_shapes=scratch,
        compiler_params=pltpu.CompilerParams(
            dimension_semantics=("arbitrary", "arbitrary", "arbitrary"), vmem_limit_bytes=VMEM_LIMIT),
        name="ffn_moe" if moe else "ffn_dense",
    )(*args)


_R_I1, _R_I2, _R_W1, _R_W2, _R_RANK1, _R_RANK2 = 0, 1, 2, 3, 4, 5
_ROW_DMA_UNROLL = 8


def _route_kernel(x_ref, g_ref, sh_ref, sc_ref, r_ref, h_ref, route_ref, cnt_ref, run_scr):
    i = pl.program_id(0)
    tm = x_ref.shape[0]

    @pl.when(i == 0)
    def _():
        run_scr[...] = jnp.zeros_like(run_scr)

    h = _rmsnorm(x_ref[...], g_ref[...]) * (1.0 + sc_ref[...]) + sh_ref[...]
    h_ref[...] = h
    lane, i_hi, i_lo, w_hi, w_lo = _top2(h, r_ref[...])
    cnt = ((lane == i_hi) | (lane == i_lo)).astype(BF16)
    rr = lax.broadcasted_iota(jnp.int32, (tm, tm), 0)
    cc = lax.broadcasted_iota(jnp.int32, (tm, tm), 1)
    before = jnp.dot((rr > cc).astype(BF16), cnt, preferred_element_type=F32) + run_scr[...]
    rank_hi = jnp.sum(jnp.where(lane == i_hi, before, 0.0), axis=1, keepdims=True)
    rank_lo = jnp.sum(jnp.where(lane == i_lo, before, 0.0), axis=1, keepdims=True)
    run_scr[...] += jnp.sum(cnt.astype(F32), axis=0, keepdims=True)
    cnt_ref[...] = run_scr[...]
    rec = jnp.zeros(lane.shape, F32)
    for ln, val in ((_R_I1, i_hi), (_R_I2, i_lo), (_R_W1, w_hi), (_R_W2, w_lo),
                    (_R_RANK1, rank_hi), (_R_RANK2, rank_lo)):
        rec = jnp.where(lane == float(ln), val, rec)
    route_ref[...] = rec


def _route(x, g2, mod, mod_spec, router, *, tm):
    n = x.shape[0]
    return pl.pallas_call(
        _route_kernel,
        grid=(n // tm,),
        in_specs=[
            pl.BlockSpec((tm, D_MODEL), lambda i: (i, 0)),
            pl.BlockSpec((1, D_MODEL), lambda i: (0, 0)),
            mod_spec(3), mod_spec(4),
            pl.BlockSpec((D_MODEL, LANES), lambda i: (0, 0)),
        ],
        out_specs=[
            pl.BlockSpec((tm, D_MODEL), lambda i: (i, 0)),
            pl.BlockSpec((tm, LANES), lambda i: (i, 0)),
            pl.BlockSpec((1, LANES), lambda i: (0, 0)),
        ],
        out_shape=[
            jax.ShapeDtypeStruct((n, D_MODEL), F32),
            jax.ShapeDtypeStruct((n, LANES), F32),
            jax.ShapeDtypeStruct((1, LANES), F32),
        ],
        scratch_shapes=[pltpu.VMEM((1, LANES), F32)],
        compiler_params=pltpu.CompilerParams(
            dimension_semantics=("arbitrary",), vmem_limit_bytes=VMEM_LIMIT),
        name="moe_route",
    )(x, g2, mod, mod, router)


def _dispatch_kernel(pos_ref, h_ref, xs_in_ref, xs_ref, sem):
    del xs_in_ref
    rows = h_ref.shape[0]

    def row_copy(r, k):
        return pltpu.make_async_copy(
            h_ref.at[pl.ds(r, 1), :], xs_ref.at[pl.ds(pos_ref[2 * r + k], 1), :], sem)

    def start(r, carry):
        row_copy(r, 0).start(priority=0)
        row_copy(r, 1).start(priority=1)
        return carry

    def wait(r, carry):
        row_copy(r, 0).wait()
        row_copy(r, 1).wait()
        return carry

    lax.fori_loop(0, rows, start, 0, unroll=_ROW_DMA_UNROLL)
    lax.fori_loop(0, rows, wait, 0, unroll=_ROW_DMA_UNROLL)


def _dispatch(h, pos, n_rows, *, tm):
    n = h.shape[0]
    xs0 = jnp.zeros((n_rows, D_MODEL), F32)
    return pl.pallas_call(
        _dispatch_kernel,
        grid=(n // tm,),
        in_specs=[
            pl.BlockSpec((2 * tm,), lambda i: (i,), memory_space=pltpu.SMEM),
            pl.BlockSpec((tm, D_MODEL), lambda i: (i, 0)),
            pl.BlockSpec(memory_space=pl.ANY),
        ],
        out_specs=pl.BlockSpec(memory_space=pl.ANY),
        out_shape=jax.ShapeDtypeStruct((n_rows, D_MODEL), F32),
        scratch_shapes=[pltpu.SemaphoreType.DMA(())],
        input_output_aliases={2: 0},
        compiler_params=pltpu.CompilerParams(
            dimension_semantics=("arbitrary",), vmem_limit_bytes=VMEM_LIMIT),
        name="moe_dispatch",
    )(pos, h, xs0)


def _gffn_kernel(te_ref, nv_ref, xs_ref, w1_ref, w3_ref, w2_ref, ys_ref):
    del te_ref
    i = pl.program_id(0)
    j = pl.program_id(1)

    @pl.when(i < nv_ref[0])
    def _():
        @pl.when(j == 0)
        def _():
            ys_ref[...] = jnp.zeros_like(ys_ref)

        h = xs_ref[...].astype(BF16)
        a = jnp.dot(h, w1_ref[...].astype(BF16), preferred_element_type=F32)
        b = jnp.dot(h, w3_ref[...].astype(BF16), preferred_element_type=F32)
        ys_ref[...] += _bdot(_silu(a) * b, w2_ref[...])

    @pl.when((i >= nv_ref[0]) & (j == 0))
    def _():
        ys_ref[...] = jnp.zeros_like(ys_ref)


def _gffn(xs, tile_expert, n_valid, w1, w3, w2, wl, *, tm, tf=1408):
    n_rows = xs.shape[0]
    last_j = D_FF // tf - 1

    def row_idx(i, j, te, nv):
        return (jnp.minimum(i, nv[0] - 1), 0)

    def ff_idx(i, j, nv):
        return jnp.where(i < nv[0], j, last_j)

    grid_spec = pltpu.PrefetchScalarGridSpec(
        num_scalar_prefetch=2,
        grid=(n_rows // tm, D_FF // tf),
        in_specs=[
            pl.BlockSpec((tm, D_MODEL), row_idx),
            pl.BlockSpec((None, None, D_MODEL, tf), lambda i, j, te, nv: (wl, te[i], 0, ff_idx(i, j, nv))),
            pl.BlockSpec((None, None, D_MODEL, tf), lambda i, j, te, nv: (wl, te[i], 0, ff_idx(i, j, nv))),
            pl.BlockSpec((None, None, tf, D_MODEL), lambda i, j, te, nv: (wl, te[i], ff_idx(i, j, nv), 0)),
        ],
        out_specs=pl.BlockSpec((tm, D_MODEL), lambda i, j, te, nv: (i, 0)),
    )
    return pl.pallas_call(
        _gffn_kernel,
        grid_spec=grid_spec,
        out_shape=jax.ShapeDtypeStruct((n_rows, D_MODEL), F32),
        compiler_params=pltpu.CompilerParams(
            dimension_semantics=("arbitrary", "arbitrary"), vmem_limit_bytes=VMEM_LIMIT),
        name="moe_grouped_ffn",
    )(tile_expert, n_valid, xs, w1, w3, w2)


def _combine_kernel(pos_ref, x_ref, gt_ref, route_ref, fg_ref, ys_ref, xo_ref, y0_scr, y1_scr, sem, *, final_norm):
    rows = x_ref.shape[0]

    def row_copy(r, k):
        buf = y0_scr if k == 0 else y1_scr
        return pltpu.make_async_copy(
            ys_ref.at[pl.ds(pos_ref[2 * r + k], 1), :], buf.at[pl.ds(r, 1), :], sem)

    def start(r, carry):
        row_copy(r, 0).start(priority=0)
        row_copy(r, 1).start(priority=1)
        return carry

    def wait(r, carry):
        row_copy(r, 0).wait()
        row_copy(r, 1).wait()
        return carry

    lax.fori_loop(0, rows, start, 0, unroll=_ROW_DMA_UNROLL)
    lax.fori_loop(0, rows, wait, 0, unroll=_ROW_DMA_UNROLL)
    route = route_ref[...]
    fo = route[:, _R_W1:_R_W1 + 1] * y0_scr[...] + route[:, _R_W2:_R_W2 + 1] * y1_scr[...]
    x_new = x_ref[...] + gt_ref[...] * fo
    xo_ref[...] = _rmsnorm(x_new, fg_ref[...]) if final_norm else x_new


def _combine(x, mod, mod_spec, route, pos, ys, final_g, *, tm):
    n = x.shape[0]
    final_norm = final_g is not None
    fg = final_g if final_norm else jnp.ones((1, D_MODEL), F32)
    return pl.pallas_call(
        functools.partial(_combine_kernel, final_norm=final_norm),
        grid=(n // tm,),
        in_specs=[
            pl.BlockSpec((2 * tm,), lambda i: (i,), memory_space=pltpu.SMEM),
            pl.BlockSpec((tm, D_MODEL), lambda i: (i, 0)),
            mod_spec(5),
            pl.BlockSpec((tm, LANES), lambda i: (i, 0)),
            pl.BlockSpec((1, D_MODEL), lambda i: (0, 0)),
            pl.BlockSpec(memory_space=pl.ANY),
        ],
        out_specs=pl.BlockSpec((tm, D_MODEL), lambda i: (i, 0)),
        out_shape=jax.ShapeDtypeStruct((n, D_MODEL), F32),
        scratch_shapes=[pltpu.VMEM((tm, D_MODEL), F32), pltpu.VMEM((tm, D_MODEL), F32),
                        pltpu.SemaphoreType.DMA(())],
        compiler_params=pltpu.CompilerParams(
            dimension_semantics=("arbitrary",), vmem_limit_bytes=VMEM_LIMIT),
        name="moe_combine",
    )(pos, x, mod, route, fg, ys)


def _moe_routed(x, g2, mod, tiles_per_seq_of, router, w1, w3, w2, wl, final_g=None, *,
                tm_route=512, tm_move=512, tm_group=512):
    n = x.shape[0]

    def spec1(tm):
        return lambda c: pl.BlockSpec((None, 1, D_MODEL), lambda i: (i // tiles_per_seq_of(tm), 0, c))

    h, route, counts = _route(x, g2, mod, spec1(tm_route), router, tm=tm_route)
    cnt = counts[0, :N_EXPERTS].astype(jnp.int32)
    tiles_e = (cnt + tm_group - 1) // tm_group
    tile_start = jnp.cumsum(tiles_e) - tiles_e
    n_valid = jnp.sum(tiles_e)
    n_tiles = (2 * n) // tm_group + N_EXPERTS
    tile_ids = jnp.minimum(jnp.arange(n_tiles, dtype=jnp.int32), n_valid - 1)
    tile_expert = (jnp.sum(tile_ids[:, None] >= tile_start[None, :], axis=1) - 1).astype(jnp.int32)
    idx = route[:, _R_I1:_R_I2 + 1].astype(jnp.int32)
    rank = route[:, _R_RANK1:_R_RANK2 + 1].astype(jnp.int32)
    first_row = jnp.sum(jnp.where(idx[..., None] == jnp.arange(N_EXPERTS, dtype=jnp.int32),
                                  tile_start * tm_group, 0), axis=-1)
    pos = (first_row + rank).reshape(2 * n)

    xs = _dispatch(h, pos, n_tiles * tm_group, tm=tm_move)
    ys = _gffn(xs, tile_expert, n_valid.reshape(1).astype(jnp.int32), w1, w3, w2, wl, tm=tm_group)
    return _combine(x, mod, spec1(tm_move), route, pos, ys, final_g, tm=tm_move)


def _final_norm_kernel(x_ref, g_ref, o_ref):
    o_ref[...] = _rmsnorm(x_ref[...], g_ref[...])


def _final_norm(x, g, *, tm):
    n = x.shape[0]
    return pl.pallas_call(
        _final_norm_kernel,
        grid=(n // tm,),
        in_specs=[pl.BlockSpec((tm, D_MODEL), lambda i: (i, 0)), pl.BlockSpec((1, D_MODEL), lambda i: (0, 0))],
        out_specs=pl.BlockSpec((tm, D_MODEL), lambda i: (i, 0)),
        out_shape=jax.ShapeDtypeStruct((n, D_MODEL), F32),
        compiler_params=pltpu.CompilerParams(
            dimension_semantics=("arbitrary",), vmem_limit_bytes=VMEM_LIMIT),
        name="final_norm",
    )(x, g)


def _block_diag(w):
    nb, bw, _ = w.shape
    eye = jnp.eye(nb, dtype=w.dtype)
    return (eye[:, None, :, None] * w[:, :, None, :]).reshape(nb * bw, nb * bw)


def kernel(x_prompt, x_sample, c_prompt, c_sample, state_rglru_h, state_rglru_conv, state_hgrn,
           mod_w, mod_b, norm1_g, norm2_g, w_in, conv_w, conv_b, rg_wa, rg_ba, rg_wx, rg_bx,
           rg_lambda, rg_out_g, hg_lb_logits, hg_norm_g, w_out, ffn_w1, ffn_w3, ffn_w2,
           router_w, moe_w1, moe_w3, moe_w2, final_g):
    n_b, seq, _ = x_prompt.shape
    n_s = x_sample.shape[0]
    tm_p = 512
    tiles_per_seq = seq // tm_p

    mod = _modulation(jnp.concatenate([c_prompt, c_sample], axis=0), mod_w, mod_b)
    lbs = _lower_bounds(hg_lb_logits)

    def row(v):
        return v.reshape(1, -1)

    w_in_b = w_in.astype(BF16)
    w_out_b = w_out.astype(BF16)
    dense_w = [w.astype(BF16)[:, None] for w in (ffn_w1, ffn_w3, ffn_w2)]
    moe_w = [w.astype(BF16) for w in (moe_w1, moe_w3, moe_w2)]
    conv_all = state_rglru_conv.reshape(DEPTH, n_s, (CONV_W - 1) * W_A)

    xp = x_prompt
    xs = x_sample.reshape(n_s, D_MODEL)
    outs = {k: [] for k in ("hp", "cp", "sp", "hs", "cs", "ss")}
    for l in range(DEPTH):
        mod_p = mod[l, :n_b].reshape(n_b, 1, 6 * D_MODEL)
        mod_s = mod[l, n_b:]
        pvec = jnp.concatenate([
            conv_w[l], row(conv_b[l]), row(rg_ba[l]), row(rg_bx[l]), row(rg_lambda[l]), row(rg_out_g[l]),
            row(lbs[l]), row(hg_norm_g[l]), jnp.zeros((_P_ROWS - 11, W_A), F32)], axis=0)
        bd_a, bd_x, half = _block_diag(rg_wa[l]), _block_diag(rg_wx[l]), W_A // 2
        wg = jnp.stack([
            jnp.concatenate([bd[hh * half:(hh + 1) * half, hh * half:(hh + 1) * half] for bd in (bd_a, bd_x)], axis=1)
            for hh in range(2)]).astype(BF16)

        xp, hp, tail, sp = _mix_prompt(xp, row(norm1_g[l]), mod_p, w_in_b, pvec, wg, w_out_b, l)
        outs["hp"].append(hp.reshape(n_b, W_A))
        outs["cp"].append(tail[:, SUBLANES - (CONV_W - 1):, :])
        outs["sp"].append(sp)

        u_s = _in_sample(xs, row(norm1_g[l]), mod_s, w_in_b, l)
        y_s, hs, cs, ss = _mix_sample(u_s, state_rglru_h, conv_all, state_hgrn, pvec, wg, l)
        xs = _out_sample(xs, y_s, mod_s, w_out_b, l)
        outs["hs"].append(hs)
        outs["cs"].append(cs.reshape(n_s, CONV_W - 1, W_A))
        outs["ss"].append(ss)

        if l % 2 == 0:
            w1, w3, w2 = dense_w
            router = None
        else:
            w1, w3, w2 = moe_w
            router = jnp.pad(router_w[l // 2], ((0, 0), (0, LANES - N_EXPERTS)))
        spec_p = lambda c: pl.BlockSpec((None, 1, D_MODEL), lambda i, e, j: (i // tiles_per_seq, 0, c))
        spec_s = lambda c: pl.BlockSpec((n_s, D_MODEL), lambda i, e, j: (0, c))
        if router is None:
            xp = _ffn(xp.reshape(n_b * seq, D_MODEL), row(norm2_g[l]), mod_p, spec_p, w1, w3, w2, l // 2, None,
                      tm=tm_p)
        else:
            xp = _moe_routed(xp.reshape(n_b * seq, D_MODEL), row(norm2_g[l]), mod_p, lambda tm: seq // tm,
                             router, w1, w3, w2, l // 2, row(final_g) if l == DEPTH - 1 else None)
        xp = xp.reshape(n_b, seq, D_MODEL)
        xs = _ffn(xs, row(norm2_g[l]), mod_s, spec_s, w1, w3, w2, l // 2, router, tm=n_s)

    y_prompt = xp
    y_sample = _final_norm(xs, row(final_g), tm=n_s).reshape(n_s, 1, D_MODEL)
    return (y_prompt, y_sample,
            jnp.stack(outs["hp"]), jnp.stack(outs["cp"]), jnp.stack(outs["sp"]),
            jnp.stack(outs["hs"]), jnp.stack(outs["cs"]), jnp.stack(outs["ss"]))
```

```python
import functools

import jax
import jax.numpy as jnp
from jax import lax
from jax.experimental import pallas as pl
from jax.experimental.pallas import tpu as pltpu

F32 = jnp.float32
BF16 = jnp.bfloat16
HIGHEST = lax.Precision.HIGHEST

D_MODEL = 1024
DEPTH = 4
W_A = 512
RG_BLOCKS = 8
CONV_W = 4
RG_C = 8.0
W_B = 512
HG_HEADS = 4
HG_D = 128
D_IN = 3072
D_FF = 2816
N_EXPERTS = 8
EPS = 1e-6
PAST_LEN = 16384

LANES = 128
SUBLANES = 8
VMEM_LIMIT = 56 * 1024 * 1024

TC_MIX = 512
HG_CHUNK = 128
TM_FFN = 512
TF_FFN = 1408
TM_MOVE = 512
TM_GROUP = 512
BS_SAMPLE = 8
TN_MOD = 1536
TN_IN = 1024

_P_CONV_W, _P_CONV_B, _P_BA, _P_BX, _P_LAM, _P_RG_G, _P_LB, _P_HG_G = 0, 4, 5, 6, 7, 8, 9, 10
_P_ROWS = 16


def _silu(x):
    return x * jax.nn.sigmoid(x)


def _gelu_tanh(x):
    cdf = 0.5 * (1.0 + jnp.tanh(0.7978845608028654 * (x + 0.044715 * (x * x * x))))
    return x * cdf


def _rmsnorm(x, g):
    return x * lax.rsqrt(jnp.mean(x * x, axis=-1, keepdims=True) + EPS) * g


def _softplus(z):
    return jnp.maximum(z, 0.0) + jnp.log1p(jnp.exp(-jnp.abs(z)))


def _bdot(a, b):
    return jnp.dot(a.astype(BF16), b.astype(BF16), preferred_element_type=F32)


def _rglru_coeffs(xc, wg, p_ref):
    half = W_A // 2
    g0 = _bdot(xc[:, :half], wg[0])
    g1 = _bdot(xc[:, half:], wg[1])
    r = jax.nn.sigmoid(jnp.concatenate([g0[:, :half], g1[:, :half]], axis=1) + p_ref[_P_BA:_P_BA + 1, :])
    i = jax.nn.sigmoid(jnp.concatenate([g0[:, half:], g1[:, half:]], axis=1) + p_ref[_P_BX:_P_BX + 1, :])
    log_a = (-RG_C * _softplus(-p_ref[_P_LAM:_P_LAM + 1, :])) * r
    a = jnp.exp(log_a)
    one_minus_a2 = -jnp.tanh(log_a) * (a * a + 1.0)
    return a, one_minus_a2, i * xc


def _sqrt_nonneg(x):
    return jnp.where(x == 0.0, 0.0, x * lax.rsqrt(x))


def _group_roll(x, k):
    r, w = x.shape
    return pltpu.roll(x.reshape(r // SUBLANES, SUBLANES, w), k, 1).reshape(r, w)


def _scan_rows(a, bb, h0):
    r, w = a.shape
    sub = lax.broadcasted_iota(jnp.int32, (r, w), 0) & (SUBLANES - 1)
    k = 1
    while k < SUBLANES:
        keep = sub >= k
        bb = a * jnp.where(keep, _group_roll(bb, k), 0.0) + bb
        a = a * jnp.where(keep, _group_roll(a, k), 1.0)
        k *= 2
    outs = []
    carry = h0
    for g in range(r // SUBLANES):
        rows = slice(g * SUBLANES, (g + 1) * SUBLANES)
        hg = a[rows] * carry + bb[rows]
        outs.append(hg)
        carry = hg[SUBLANES - 1:SUBLANES, :]
    return jnp.concatenate(outs, axis=0)


def _head_rmsnorm_gate(o, gb, g_row):
    outs = []
    for h in range(HG_HEADS):
        hs = slice(h * HG_D, (h + 1) * HG_D)
        outs.append(_rmsnorm(o[:, hs], g_row[:, hs]))
    return jnp.concatenate(outs, axis=1) * _silu(gb)


def _mod_kernel(c_ref, w_ref, b_ref, o_ref):
    o_ref[...] = _bdot(_silu(c_ref[...]), w_ref[...]) + b_ref[...]


def _modulation(c_all, mod_w, mod_b):
    rows = c_all.shape[0]
    tn = TN_MOD
    n_out = mod_w.shape[-1]
    return pl.pallas_call(
        _mod_kernel,
        grid=(DEPTH, n_out // tn),
        in_specs=[
            pl.BlockSpec((rows, D_MODEL), lambda l, j: (0, 0)),
            pl.BlockSpec((None, D_MODEL, tn), lambda l, j: (l, 0, j)),
            pl.BlockSpec((None, 1, tn), lambda l, j: (l, 0, j)),
        ],
        out_specs=pl.BlockSpec((None, rows, tn), lambda l, j: (l, 0, j)),
        out_shape=jax.ShapeDtypeStruct((DEPTH, rows, n_out), F32),
        compiler_params=pltpu.CompilerParams(
            dimension_semantics=("arbitrary", "arbitrary"), vmem_limit_bytes=VMEM_LIMIT),
        name="adaln_mod",
    )(c_all, mod_w, mod_b.reshape(DEPTH, 1, n_out))


def _lb_kernel(l_ref, o_ref):
    x = l_ref[...]
    e = jnp.exp(x - jnp.max(x, axis=0, keepdims=True))
    p = e / jnp.sum(e, axis=0, keepdims=True)
    p0 = p[0:1, :]
    run = p0
    o_ref[0:1, :] = run - p0
    for r in range(1, DEPTH):
        run = run + p[r:r + 1, :]
        o_ref[r:r + 1, :] = run - p0


def _lower_bounds(hg_lb_logits):
    return pl.pallas_call(
        _lb_kernel,
        out_shape=jax.ShapeDtypeStruct(hg_lb_logits.shape, F32),
        name="hgrn_lower_bounds",
    )(hg_lb_logits)


def _hgrn_chunk(q, k, v, f, logf, st):
    C = q.shape[0]
    G = C // SUBLANES
    rc = lax.broadcasted_iota(jnp.int32, (C, C), 0)
    cc = lax.broadcasted_iota(jnp.int32, (C, C), 1)
    sr = lax.broadcasted_iota(jnp.int32, (C + 2 * G, C), 0)
    sc = lax.broadcasted_iota(jnp.int32, (C + 2 * G, C), 1)
    bound = jnp.where(sr < C, sr + 1,
                      jnp.where(sr < C + G, (sr - C) * SUBLANES, (sr - C - G + 1) * SUBLANES))
    hi = logf.astype(BF16)
    rest = logf - hi.astype(F32)
    mid = rest.astype(BF16)
    lo = (rest - mid.astype(F32)).astype(BF16)
    sums = jnp.dot((sc < bound).astype(BF16), jnp.concatenate([hi, mid, lo], axis=1), preferred_element_type=F32)
    sums = (sums[:, :W_B] + sums[:, W_B:2 * W_B]) + sums[:, 2 * W_B:]
    b, g_start, g_end = sums[:C], sums[C:C + G], sums[C + G:]
    row = lax.broadcasted_iota(jnp.int32, (C, W_B), 0)
    xor = rc ^ cc

    heads = [slice(h * HG_D, (h + 1) * HG_D) for h in range(HG_HEADS)]
    nt = (((1,), (1,)), ((), ()))
    qb, kb = q.astype(BF16), k.astype(BF16)
    att = [jnp.where(rc == cc, lax.dot_general(qb[:, hs], kb[:, hs], nt, preferred_element_type=F32), 0.0)
           for hs in heads]

    def add_level(x, m, lg):
        mask = ((xor >> lg) == 1) & ((rc & m) != 0)
        for h, hs in enumerate(heads):
            p = lax.dot_general(x[:, hs], x[:, hs], nt, preferred_element_type=F32)
            att[h] = jnp.where(mask, p, att[h])

    c = b - logf
    d = b
    m, lg = 1, 0
    while m < SUBLANES:
        odd = (row & m) != 0
        if m == 1:
            x = jnp.where(odd, q * f, k)
        else:
            x = jnp.where(odd, q, k) * jnp.exp(jnp.where(odd, b - c, d - b))
        add_level(x.astype(BF16), m, lg)
        c = jnp.where(odd, _group_roll(c, m), c)
        d = jnp.where(odd, d, _group_roll(d, SUBLANES - m))
        m, lg = 2 * m, lg + 1

    q8 = q * jnp.exp(b - c)
    k8 = k * jnp.exp(d - b)
    grow = lax.broadcasted_iota(jnp.int32, (G, W_B), 0)
    cg, dg = g_start, g_end
    mu = 1
    while mu < G:
        oddg = (grow & mu) != 0
        rg = jnp.exp(jnp.where(oddg, g_start - cg, dg - g_end))
        pieces = []
        for g in range(G):
            base = q8 if (g // mu) % 2 == 1 else k8
            pieces.append(base[g * SUBLANES:(g + 1) * SUBLANES] * rg[g:g + 1, :])
        add_level(jnp.concatenate(pieces, axis=0).astype(BF16), m, lg)
        cg = jnp.where(oddg, pltpu.roll(cg, mu, 0), cg)
        dg = jnp.where(oddg, dg, pltpu.roll(dg, G - mu, 0))
        mu, m, lg = 2 * mu, 2 * m, lg + 1

    b_last = b[C - 1:C, :]
    qe = (q * jnp.exp(b)).astype(BF16)
    kd = (k * jnp.exp(b_last - b)).astype(BF16)
    vb = v.astype(BF16)
    tn = (((0,), (0,)), ((), ()))
    outs, new_st = [], []
    for h, hs in enumerate(heads):
        o_inter = lax.dot_general(qe[:, hs], st[h].astype(BF16), nt, preferred_element_type=F32)
        o_intra = jnp.dot(att[h].astype(BF16), vb[:, hs], preferred_element_type=F32)
        outs.append(o_inter + o_intra)
        upd = lax.dot_general(vb[:, hs], kd[:, hs], tn, preferred_element_type=F32)
        new_st.append(jnp.exp(b_last[:, hs]) * st[h] + upd)
    return jnp.concatenate(outs, axis=1), new_st


def _mix_prompt_kernel(x_ref, g1_ref, sh_ref, sc_ref, gt_ref, win_ref, p_ref, wg_ref, wout_ref,
                       xo_ref, hn_ref, tail_ref, sn_ref,
                       hcar, ext, st_scr, *, hg_chunk):
    t = pl.program_id(1)
    tc = x_ref.shape[0]

    @pl.when(t == 0)
    def _():
        hcar[...] = jnp.zeros_like(hcar)
        ext[0:SUBLANES, :] = jnp.zeros((SUBLANES, W_A), F32)
        st_scr[...] = jnp.zeros_like(st_scr)

    x = x_ref[...]
    h = _rmsnorm(x, g1_ref[...]) * (1.0 + sc_ref[...]) + sh_ref[...]
    u = _bdot(h, win_ref[...])

    xa = u[:, 0:W_A]
    ga = u[:, W_A:2 * W_A]
    ext[SUBLANES:, :] = xa
    xc = p_ref[_P_CONV_B:_P_CONV_B + 1, :]
    for j in range(CONV_W):
        lo = SUBLANES - (CONV_W - 1) + j
        xc = xc + ext[lo:lo + tc, :] * p_ref[_P_CONV_W + j:_P_CONV_W + j + 1, :]
    tail = xa[tc - SUBLANES:, :]
    ext[0:SUBLANES, :] = tail
    tail_ref[...] = tail

    a, one_minus_a2, ix = _rglru_coeffs(xc, wg_ref, p_ref)
    row = lax.broadcasted_iota(jnp.int32, (tc, W_A), 0)
    first = (row == 0) & (t == 0)
    a = jnp.where(first, 0.0, a)
    bb = jnp.where(first, 1.0, _sqrt_nonneg(one_minus_a2)) * ix
    hseq = _scan_rows(a, bb, hcar[...])
    h_last = hseq[tc - 1:tc, :]
    hcar[...] = h_last
    hn_ref[...] = h_last
    y_a = _rmsnorm(hseq * _gelu_tanh(ga), p_ref[_P_RG_G:_P_RG_G + 1, :])

    lb = p_ref[_P_LB:_P_LB + 1, :]
    f = lb + (1.0 - lb) * jax.nn.sigmoid(u[:, 3 * W_A:4 * W_A])
    q = _silu(u[:, 2 * W_A:3 * W_A])
    kk = 1.0 - f
    logf = jnp.log(f)
    v = u[:, 4 * W_A:5 * W_A]
    st = [st_scr[hh] for hh in range(HG_HEADS)]
    outs = []
    for c0 in range(0, tc, hg_chunk):
        cs = slice(c0, c0 + hg_chunk)
        o_c, st = _hgrn_chunk(q[cs], kk[cs], v[cs], f[cs], logf[cs], st)
        outs.append(o_c)
    o = outs[0] if len(outs) == 1 else jnp.concatenate(outs, axis=0)
    for hh in range(HG_HEADS):
        st_scr[hh] = st[hh]
        sn_ref[hh] = st[hh].T
    y_b = _head_rmsnorm_gate(o, u[:, 5 * W_A:6 * W_A], p_ref[_P_HG_G:_P_HG_G + 1, :])

    y = _bdot(jnp.concatenate([y_a, y_b], axis=1), wout_ref[...])
    xo_ref[...] = x + gt_ref[...] * y


def _mix_prompt(x, g1, mod, w_in, pvec, wg, w_out, layer, *, tc=TC_MIX, hg_chunk=HG_CHUNK):
    bsz, seq, _ = x.shape
    const = lambda b, t: (0, 0)
    wconst = lambda b, t: (layer, 0, 0)
    kern = functools.partial(_mix_prompt_kernel, hg_chunk=hg_chunk)
    return pl.pallas_call(
        kern,
        grid=(bsz, seq // tc),
        in_specs=[
            pl.BlockSpec((None, tc, D_MODEL), lambda b, t: (b, t, 0)),
            pl.BlockSpec((1, D_MODEL), const),
            pl.BlockSpec((None, 1, D_MODEL), lambda b, t: (b, 0, 0)),
            pl.BlockSpec((None, 1, D_MODEL), lambda b, t: (b, 0, 1)),
            pl.BlockSpec((None, 1, D_MODEL), lambda b, t: (b, 0, 2)),
            pl.BlockSpec((None, D_MODEL, D_IN), wconst),
            pl.BlockSpec((_P_ROWS, W_A), const),
            pl.BlockSpec((2, W_A // 2, W_A), lambda b, t: (0, 0, 0)),
            pl.BlockSpec((None, D_MODEL, D_MODEL), wconst),
        ],
        out_specs=[
            pl.BlockSpec((None, tc, D_MODEL), lambda b, t: (b, t, 0)),
            pl.BlockSpec((None, 1, W_A), lambda b, t: (b, 0, 0)),
            pl.BlockSpec((None, SUBLANES, W_A), lambda b, t: (b, 0, 0)),
            pl.BlockSpec((None, HG_HEADS, HG_D, HG_D), lambda b, t: (b, 0, 0, 0)),
        ],
        out_shape=[
            jax.ShapeDtypeStruct(x.shape, F32),
            jax.ShapeDtypeStruct((bsz, 1, W_A), F32),
            jax.ShapeDtypeStruct((bsz, SUBLANES, W_A), F32),
            jax.ShapeDtypeStruct((bsz, HG_HEADS, HG_D, HG_D), F32),
        ],
        scratch_shapes=[
            pltpu.VMEM((1, W_A), F32),
            pltpu.VMEM((tc + SUBLANES, W_A), F32),
            pltpu.VMEM((HG_HEADS, HG_D, HG_D), F32),
        ],
        compiler_params=pltpu.CompilerParams(
            dimension_semantics=("arbitrary", "arbitrary"), vmem_limit_bytes=VMEM_LIMIT),
        name="mix_prompt",
    )(x, g1, mod, mod, mod, w_in, pvec, wg, w_out)


def _in_sample_kernel(x_ref, g1_ref, sh_ref, sc_ref, win_ref, u_ref):
    h = _rmsnorm(x_ref[...], g1_ref[...]) * (1.0 + sc_ref[...]) + sh_ref[...]
    u_ref[...] = _bdot(h, win_ref[...])


def _in_sample(x, g1, mod, w_in, layer):
    n = x.shape[0]
    tn = TN_IN
    return pl.pallas_call(
        _in_sample_kernel,
        grid=(D_IN // tn,),
        in_specs=[
            pl.BlockSpec((n, D_MODEL), lambda j: (0, 0)),
            pl.BlockSpec((1, D_MODEL), lambda j: (0, 0)),
            pl.BlockSpec((n, D_MODEL), lambda j: (0, 0)),
            pl.BlockSpec((n, D_MODEL), lambda j: (0, 1)),
            pl.BlockSpec((None, D_MODEL, tn), lambda j: (layer, 0, j)),
        ],
        out_specs=pl.BlockSpec((n, tn), lambda j: (0, j)),
        out_shape=jax.ShapeDtypeStruct((n, D_IN), F32),
        compiler_params=pltpu.CompilerParams(
            dimension_semantics=("arbitrary",), vmem_limit_bytes=VMEM_LIMIT),
        name="in_sample",
    )(x, g1, mod, mod, w_in)


def _mix_sample_kernel(u_ref, h0_ref, c0_ref, s0_ref, p_ref, wg_ref,
                       y_ref, hn_ref, cn_ref, sn_ref, o_scr):
    bs = u_ref.shape[0]
    u = u_ref[...]
    xa = u[:, 0:W_A]
    ga = u[:, W_A:2 * W_A]
    c0 = c0_ref[...]
    xc = p_ref[_P_CONV_B:_P_CONV_B + 1, :]
    for j in range(CONV_W - 1):
        xc = xc + c0[:, j * W_A:(j + 1) * W_A] * p_ref[_P_CONV_W + j:_P_CONV_W + j + 1, :]
    xc = xc + xa * p_ref[_P_CONV_W + CONV_W - 1:_P_CONV_W + CONV_W, :]
    cn_ref[:, 0:2 * W_A] = c0[:, W_A:]
    cn_ref[:, 2 * W_A:] = xa

    a, one_minus_a2, ix = _rglru_coeffs(xc, wg_ref, p_ref)
    hnew = _sqrt_nonneg(one_minus_a2) * ix + a * h0_ref[...]
    hn_ref[...] = hnew
    y_a = _rmsnorm(hnew * _gelu_tanh(ga), p_ref[_P_RG_G:_P_RG_G + 1, :])

    lb = p_ref[_P_LB:_P_LB + 1, :]
    f = lb + (1.0 - lb) * jax.nn.sigmoid(u[:, 3 * W_A:4 * W_A])
    q = _silu(u[:, 2 * W_A:3 * W_A])
    v = u[:, 4 * W_A:5 * W_A]
    eye = (lax.broadcasted_iota(jnp.int32, (bs, bs), 0) == lax.broadcasted_iota(jnp.int32, (bs, bs), 1)).astype(F32)
    fcols = lax.dot_general(f, eye, (((0,), (0,)), ((), ())), precision=HIGHEST, preferred_element_type=F32)
    qb = q.astype(BF16)
    for j in range(bs):
        for hh in range(HG_HEADS):
            hs = slice(hh * HG_D, (hh + 1) * HG_D)
            fb = jnp.broadcast_to(fcols[hh * HG_D:(hh + 1) * HG_D, j:j + 1], (HG_D, HG_D))
            s_new = fb * s0_ref[j, hh] + (1.0 - fb) * v[j:j + 1, hs]
            sn_ref[j, hh] = s_new
            q_rows = jnp.broadcast_to(qb[j:j + 1, hs], (SUBLANES, HG_D))
            o_rows = jnp.dot(q_rows, s_new.astype(BF16), preferred_element_type=F32)
            o_scr[j:j + 1, hs] = o_rows[0:1, :]
    y_b = _head_rmsnorm_gate(o_scr[...], u[:, 5 * W_A:6 * W_A], p_ref[_P_HG_G:_P_HG_G + 1, :])
    y_ref[:, 0:W_A] = y_a
    y_ref[:, W_A:] = y_b


def _mix_sample(u, h0, c0, s0, pvec, wg, layer, *, bs=BS_SAMPLE):
    n = u.shape[0]
    const = lambda i: (0, 0)
    return pl.pallas_call(
        _mix_sample_kernel,
        grid=(n // bs,),
        in_specs=[
            pl.BlockSpec((bs, D_IN), lambda i: (i, 0)),
            pl.BlockSpec((None, bs, W_A), lambda i: (layer, i, 0)),
            pl.BlockSpec((None, bs, (CONV_W - 1) * W_A), lambda i: (layer, i, 0)),
            pl.BlockSpec((None, bs, HG_HEADS, HG_D, HG_D), lambda i: (layer, i, 0, 0, 0)),
            pl.BlockSpec((_P_ROWS, W_A), const),
            pl.BlockSpec((2, W_A // 2, W_A), lambda i: (0, 0, 0)),
        ],
        out_specs=[
            pl.BlockSpec((bs, D_MODEL), lambda i: (i, 0)),
            pl.BlockSpec((bs, W_A), lambda i: (i, 0)),
            pl.BlockSpec((bs, (CONV_W - 1) * W_A), lambda i: (i, 0)),
            pl.BlockSpec((bs, HG_HEADS, HG_D, HG_D), lambda i: (i, 0, 0, 0)),
        ],
        out_shape=[
            jax.ShapeDtypeStruct((n, D_MODEL), F32),
            jax.ShapeDtypeStruct((n, W_A), F32),
            jax.ShapeDtypeStruct((n, (CONV_W - 1) * W_A), F32),
            jax.ShapeDtypeStruct((n, HG_HEADS, HG_D, HG_D), F32),
        ],
        scratch_shapes=[pltpu.VMEM((bs, W_B), F32)],
        compiler_params=pltpu.CompilerParams(
            dimension_semantics=("arbitrary",), vmem_limit_bytes=VMEM_LIMIT),
        name="mix_sample",
    )(u, h0, c0, s0, pvec, wg)


def _out_sample_kernel(x_ref, y_ref, gt_ref, wout_ref, xo_ref):
    xo_ref[...] = x_ref[...] + gt_ref[...] * _bdot(y_ref[...], wout_ref[...])


def _out_sample(x, y, mod, w_out, layer):
    n = x.shape[0]
    return pl.pallas_call(
        _out_sample_kernel,
        grid=(1,),
        in_specs=[
            pl.BlockSpec((n, D_MODEL), lambda i: (0, 0)),
            pl.BlockSpec((n, D_MODEL), lambda i: (0, 0)),
            pl.BlockSpec((n, D_MODEL), lambda i: (0, 2)),
            pl.BlockSpec((None, D_MODEL, D_MODEL), lambda i: (layer, 0, 0)),
        ],
        out_specs=pl.BlockSpec((n, D_MODEL), lambda i: (0, 0)),
        out_shape=jax.ShapeDtypeStruct((n, D_MODEL), F32),
        compiler_params=pltpu.CompilerParams(
            dimension_semantics=("arbitrary",), vmem_limit_bytes=VMEM_LIMIT),
        name="out_sample",
    )(x, y, mod, w_out)


def _top2(h, router):
    h_hi = h.astype(BF16)
    h_lo = (h - h_hi.astype(F32)).astype(BF16)
    r_hi = router.astype(BF16)
    r_lo = (router - r_hi.astype(F32)).astype(BF16)
    logits = (jnp.dot(h_hi, r_hi, preferred_element_type=F32) + jnp.dot(h_hi, r_lo, preferred_element_type=F32)
              ) + jnp.dot(h_lo, r_hi, preferred_element_type=F32)
    lane = lax.broadcasted_iota(jnp.int32, logits.shape, 1).astype(F32)
    neg = -jnp.inf
    lg = jnp.where(lane < N_EXPERTS, logits, neg)
    m1 = jnp.max(lg, axis=1, keepdims=True)
    i1 = jnp.min(jnp.where(lg == m1, lane, float(LANES)), axis=1, keepdims=True)
    lg2 = jnp.where(lane == i1, neg, lg)
    m2 = jnp.max(lg2, axis=1, keepdims=True)
    i2 = jnp.min(jnp.where(lg2 == m2, lane, float(LANES)), axis=1, keepdims=True)
    e2 = jnp.exp(m2 - m1)
    den = 1.0 + e2
    return lane, i1, i2, 1.0 / den, e2 / den


def _top2_gates(h, router):
    lane, i1, i2, w1, w2 = _top2(h, router)
    return jnp.where(lane == i1, w1, 0.0) + jnp.where(lane == i2, w2, 0.0)


def _ffn_kernel(*refs, moe):
    if moe:
        (x_ref, g_ref, sh_ref, sc_ref, gt_ref, r_ref, w1_ref, w3_ref, w2_ref,
         xo_ref, h_scr, acc_scr, gate_scr) = refs
    else:
        (x_ref, g_ref, sh_ref, sc_ref, gt_ref, w1_ref, w3_ref, w2_ref,
         xo_ref, h_scr, acc_scr) = refs
    e = pl.program_id(1)
    j = pl.program_id(2)

    @pl.when((e == 0) & (j == 0))
    def _():
        h = _rmsnorm(x_ref[...], g_ref[...]) * (1.0 + sc_ref[...]) + sh_ref[...]
        h_scr[...] = h.astype(BF16)
        acc_scr[...] = jnp.zeros_like(acc_scr)
        if moe:
            gate_scr[...] = _top2_gates(h, r_ref[...])

    h = h_scr[...]
    a = jnp.dot(h, w1_ref[...].astype(BF16), preferred_element_type=F32)
    b = jnp.dot(h, w3_ref[...].astype(BF16), preferred_element_type=F32)
    part = _bdot(_silu(a) * b, w2_ref[...])
    if moe:
        lane = lax.broadcasted_iota(jnp.int32, gate_scr.shape, 1)
        part = part * jnp.sum(jnp.where(lane == e, gate_scr[...], 0.0), axis=1, keepdims=True)
    acc_scr[...] += part

    @pl.when((e == pl.num_programs(1) - 1) & (j == pl.num_programs(2) - 1))
    def _():
        xo_ref[...] = x_ref[...] + gt_ref[...] * acc_scr[...]


def _ffn(x, g2, mod, mod_spec, w1, w3, w2, wl, router=None, *, tm, tf=TF_FFN):
    n = x.shape[0]
    n_exp = w1.shape[1]
    moe = router is not None
    in_specs = [
        pl.BlockSpec((tm, D_MODEL), lambda i, e, j: (i, 0)),
        pl.BlockSpec((1, D_MODEL), lambda i, e, j: (0, 0)),
        mod_spec(3), mod_spec(4), mod_spec(5),
    ]
    args = [x, g2, mod, mod, mod]
    scratch = [pltpu.VMEM((tm, D_MODEL), BF16), pltpu.VMEM((tm, D_MODEL), F32)]
    if moe:
        in_specs.append(pl.BlockSpec((D_MODEL, LANES), lambda i, e, j: (0, 0)))
        args.append(router)
        scratch.append(pltpu.VMEM((tm, LANES), F32))
    in_specs += [
        pl.BlockSpec((None, None, D_MODEL, tf), lambda i, e, j: (wl, e, 0, j)),
        pl.BlockSpec((None, None, D_MODEL, tf), lambda i, e, j: (wl, e, 0, j)),
        pl.BlockSpec((None, None, tf, D_MODEL), lambda i, e, j: (wl, e, j, 0)),
    ]
    args += [w1, w3, w2]
    return pl.pallas_call(
        functools.partial(_ffn_kernel, moe=moe),
        grid=(n // tm, n_exp, D_FF // tf),
        in_specs=in_specs,
        out_specs=pl.BlockSpec((tm, D_MODEL), lambda i, e, j: (i, 0)),
        out_shape=jax.ShapeDtypeStruct((n, D_MODEL), F32),
        scratch_shapes=scratch,
        compiler_params=pltpu.CompilerParams(
            dimension_semantics=("arbitrary", "arbitrary", "arbitrary"), vmem_limit_bytes=VMEM_LIMIT),
        name="ffn_moe" if moe else "ffn_dense",
    )(*args)


_R_I1, _R_I2, _R_W1, _R_W2, _R_RANK1, _R_RANK2 = 0, 1, 2, 3, 4, 5
_ROW_DMA_UNROLL = 8


def _route_kernel(x_ref, g_ref, sh_ref, sc_ref, r_ref, h_ref, route_ref, cnt_ref, run_scr):
    i = pl.program_id(0)
    tm = x_ref.shape[0]

    @pl.when(i == 0)
    def _():
        run_scr[...] = jnp.zeros_like(run_scr)

    h = _rmsnorm(x_ref[...], g_ref[...]) * (1.0 + sc_ref[...]) + sh_ref[...]
    h_ref[...] = h
    lane, i_hi, i_lo, w_hi, w_lo = _top2(h, r_ref[...])
    cnt = ((lane == i_hi) | (lane == i_lo)).astype(BF16)
    rr = lax.broadcasted_iota(jnp.int32, (tm, tm), 0)
    cc = lax.broadcasted_iota(jnp.int32, (tm, tm), 1)
    before = jnp.dot((rr > cc).astype(BF16), cnt, preferred_element_type=F32) + run_scr[...]
    rank_hi = jnp.sum(jnp.where(lane == i_hi, before, 0.0), axis=1, keepdims=True)
    rank_lo = jnp.sum(jnp.where(lane == i_lo, before, 0.0), axis=1, keepdims=True)
    run_scr[...] += jnp.sum(cnt.astype(F32), axis=0, keepdims=True)
    cnt_ref[...] = run_scr[...]
    rec = jnp.zeros(lane.shape, F32)
    for ln, val in ((_R_I1, i_hi), (_R_I2, i_lo), (_R_W1, w_hi), (_R_W2, w_lo),
                    (_R_RANK1, rank_hi), (_R_RANK2, rank_lo)):
        rec = jnp.where(lane == float(ln), val, rec)
    route_ref[...] = rec


def _route(x, g2, mod, mod_spec, router, *, tm):
    n = x.shape[0]
    return pl.pallas_call(
        _route_kernel,
        grid=(n // tm,),
        in_specs=[
            pl.BlockSpec((tm, D_MODEL), lambda i: (i, 0)),
            pl.BlockSpec((1, D_MODEL), lambda i: (0, 0)),
            mod_spec(3), mod_spec(4),
            pl.BlockSpec((D_MODEL, LANES), lambda i: (0, 0)),
        ],
        out_specs=[
            pl.BlockSpec((tm, D_MODEL), lambda i: (i, 0)),
            pl.BlockSpec((tm, LANES), lambda i: (i, 0)),
            pl.BlockSpec((1, LANES), lambda i: (0, 0)),
        ],
        out_shape=[
            jax.ShapeDtypeStruct((n, D_MODEL), F32),
            jax.ShapeDtypeStruct((n, LANES), F32),
            jax.ShapeDtypeStruct((1, LANES), F32),
        ],
        scratch_shapes=[pltpu.VMEM((1, LANES), F32)],
        compiler_params=pltpu.CompilerParams(
            dimension_semantics=("arbitrary",), vmem_limit_bytes=VMEM_LIMIT),
        name="moe_route",
    )(x, g2, mod, mod, router)


def _dispatch_kernel(pos_ref, h_ref, xs_in_ref, xs_ref, sem):
    del xs_in_ref
    rows = h_ref.shape[0]

    def row_copy(r, k):
        return pltpu.make_async_copy(
            h_ref.at[pl.ds(r, 1), :], xs_ref.at[pl.ds(pos_ref[2 * r + k], 1), :], sem)

    def start(r, carry):
        row_copy(r, 0).start(priority=0)
        row_copy(r, 1).start(priority=1)
        return carry

    def wait(r, carry):
        row_copy(r, 0).wait()
        row_copy(r, 1).wait()
        return carry

    lax.fori_loop(0, rows, start, 0, unroll=_ROW_DMA_UNROLL)
    lax.fori_loop(0, rows, wait, 0, unroll=_ROW_DMA_UNROLL)


def _dispatch(h, pos, n_rows, *, tm):
    n = h.shape[0]
    xs0 = jnp.zeros((n_rows, D_MODEL), F32)
    return pl.pallas_call(
        _dispatch_kernel,
        grid=(n // tm,),
        in_specs=[
            pl.BlockSpec((2 * tm,), lambda i: (i,), memory_space=pltpu.SMEM),
            pl.BlockSpec((tm, D_MODEL), lambda i: (i, 0)),
            pl.BlockSpec(memory_space=pl.ANY),
        ],
        out_specs=pl.BlockSpec(memory_space=pl.ANY),
        out_shape=jax.ShapeDtypeStruct((n_rows, D_MODEL), F32),
        scratch_shapes=[pltpu.SemaphoreType.DMA(())],
        input_output_aliases={2: 0},
        compiler_params=pltpu.CompilerParams(
            dimension_semantics=("arbitrary",), vmem_limit_bytes=VMEM_LIMIT),
        name="moe_dispatch",
    )(pos, h, xs0)


def _gffn_kernel(te_ref, nv_ref, xs_ref, w1_ref, w3_ref, w2_ref, ys_ref):
    del te_ref
    i = pl.program_id(0)
    j = pl.program_id(1)

    @pl.when(i < nv_ref[0])
    def _():
        @pl.when(j == 0)
        def _():
            ys_ref[...] = jnp.zeros_like(ys_ref)

        h = xs_ref[...].astype(BF16)
        a = jnp.dot(h, w1_ref[...].astype(BF16), preferred_element_type=F32)
        b = jnp.dot(h, w3_ref[...].astype(BF16), preferred_element_type=F32)
        ys_ref[...] += _bdot(_silu(a) * b, w2_ref[...])

    @pl.when((i >= nv_ref[0]) & (j == 0))
    def _():
        ys_ref[...] = jnp.zeros_like(ys_ref)


def _gffn(xs, tile_expert, n_valid, w1, w3, w2, wl, *, tm, tf=TF_FFN):
    n_rows = xs.shape[0]
    last_j = D_FF // tf - 1

    def row_idx(i, j, te, nv):
        return (jnp.minimum(i, nv[0] - 1), 0)

    def ff_idx(i, j, nv):
        return jnp.where(i < nv[0], j, last_j)

    grid_spec = pltpu.PrefetchScalarGridSpec(
        num_scalar_prefetch=2,
        grid=(n_rows // tm, D_FF // tf),
        in_specs=[
            pl.BlockSpec((tm, D_MODEL), row_idx),
            pl.BlockSpec((None, None, D_MODEL, tf), lambda i, j, te, nv: (wl, te[i], 0, ff_idx(i, j, nv))),
            pl.BlockSpec((None, None, D_MODEL, tf), lambda i, j, te, nv: (wl, te[i], 0, ff_idx(i, j, nv))),
            pl.BlockSpec((None, None, tf, D_MODEL), lambda i, j, te, nv: (wl, te[i], ff_idx(i, j, nv), 0)),
        ],
        out_specs=pl.BlockSpec((tm, D_MODEL), lambda i, j, te, nv: (i, 0)),
    )
    return pl.pallas_call(
        _gffn_kernel,
        grid_spec=grid_spec,
        out_shape=jax.ShapeDtypeStruct((n_rows, D_MODEL), F32),
        compiler_params=pltpu.CompilerParams(
            dimension_semantics=("arbitrary", "arbitrary"), vmem_limit_bytes=VMEM_LIMIT),
        name="moe_grouped_ffn",
    )(tile_expert, n_valid, xs, w1, w3, w2)


def _combine_kernel(pos_ref, x_ref, gt_ref, route_ref, fg_ref, ys_ref, xo_ref, y0_scr, y1_scr, sem, *, final_norm):
    rows = x_ref.shape[0]

    def row_copy(r, k):
        buf = y0_scr if k == 0 else y1_scr
        return pltpu.make_async_copy(
            ys_ref.at[pl.ds(pos_ref[2 * r + k], 1), :], buf.at[pl.ds(r, 1), :], sem)

    def start(r, carry):
        row_copy(r, 0).start(priority=0)
        row_copy(r, 1).start(priority=1)
        return carry

    def wait(r, carry):
        row_copy(r, 0).wait()
        row_copy(r, 1).wait()
        return carry

    lax.fori_loop(0, rows, start, 0, unroll=_ROW_DMA_UNROLL)
    lax.fori_loop(0, rows, wait, 0, unroll=_ROW_DMA_UNROLL)
    route = route_ref[...]
    fo = route[:, _R_W1:_R_W1 + 1] * y0_scr[...] + route[:, _R_W2:_R_W2 + 1] * y1_scr[...]
    x_new = x_ref[...] + gt_ref[...] * fo
    xo_ref[...] = _rmsnorm(x_new, fg_ref[...]) if final_norm else x_new


def _combine(x, mod, mod_spec, route, pos, ys, final_g, *, tm):
    n = x.shape[0]
    final_norm = final_g is not None
    fg = final_g if final_norm else jnp.ones((1, D_MODEL), F32)
    return pl.pallas_call(
        functools.partial(_combine_kernel, final_norm=final_norm),
        grid=(n // tm,),
        in_specs=[
            pl.BlockSpec((2 * tm,), lambda i: (i,), memory_space=pltpu.SMEM),
            pl.BlockSpec((tm, D_MODEL), lambda i: (i, 0)),
            mod_spec(5),
            pl.BlockSpec((tm, LANES), lambda i: (i, 0)),
            pl.BlockSpec((1, D_MODEL), lambda i: (0, 0)),
            pl.BlockSpec(memory_space=pl.ANY),
        ],
        out_specs=pl.BlockSpec((tm, D_MODEL), lambda i: (i, 0)),
        out_shape=jax.ShapeDtypeStruct((n, D_MODEL), F32),
        scratch_shapes=[pltpu.VMEM((tm, D_MODEL), F32), pltpu.VMEM((tm, D_MODEL), F32),
                        pltpu.SemaphoreType.DMA(())],
        compiler_params=pltpu.CompilerParams(
            dimension_semantics=("arbitrary",), vmem_limit_bytes=VMEM_LIMIT),
        name="moe_combine",
    )(pos, x, mod, route, fg, ys)


def _moe_routed(x, g2, mod, tiles_per_seq_of, router, w1, w3, w2, wl, final_g=None, *,
                tm_route=TM_FFN, tm_move=TM_MOVE, tm_group=TM_GROUP):
    n = x.shape[0]

    def spec1(tm):
        return lambda c: pl.BlockSpec((None, 1, D_MODEL), lambda i: (i // tiles_per_seq_of(tm), 0, c))

    h, route, counts = _route(x, g2, mod, spec1(tm_route), router, tm=tm_route)
    cnt = counts[0, :N_EXPERTS].astype(jnp.int32)
    tiles_e = (cnt + tm_group - 1) // tm_group
    tile_start = jnp.cumsum(tiles_e) - tiles_e
    n_valid = jnp.sum(tiles_e)
    n_tiles = (2 * n) // tm_group + N_EXPERTS
    tile_ids = jnp.minimum(jnp.arange(n_tiles, dtype=jnp.int32), n_valid - 1)
    tile_expert = (jnp.sum(tile_ids[:, None] >= tile_start[None, :], axis=1) - 1).astype(jnp.int32)
    idx = route[:, _R_I1:_R_I2 + 1].astype(jnp.int32)
    rank = route[:, _R_RANK1:_R_RANK2 + 1].astype(jnp.int32)
    first_row = jnp.sum(jnp.where(idx[..., None] == jnp.arange(N_EXPERTS, dtype=jnp.int32),
                                  tile_start * tm_group, 0), axis=-1)
    pos = (first_row + rank).reshape(2 * n)

    xs = _dispatch(h, pos, n_tiles * tm_group, tm=tm_move)
    ys = _gffn(xs, tile_expert, n_valid.reshape(1).astype(jnp.int32), w1, w3, w2, wl, tm=tm_group)
    return _combine(x, mod, spec1(tm_move), route, pos, ys, final_g, tm=tm_move)


def _final_norm_kernel(x_ref, g_ref, o_ref):
    o_ref[...] = _rmsnorm(x_ref[...], g_ref[...])


def _final_norm(x, g, *, tm):
    n = x.shape[0]
    return pl.pallas_call(
        _final_norm_kernel,
        grid=(n // tm,),
        in_specs=[pl.BlockSpec((tm, D_MODEL), lambda i: (i, 0)), pl.BlockSpec((1, D_MODEL), lambda i: (0, 0))],
        out_specs=pl.BlockSpec((tm, D_MODEL), lambda i: (i, 0)),
        out_shape=jax.ShapeDtypeStruct((n, D_MODEL), F32),
        compiler_params=pltpu.CompilerParams(
            dimension_semantics=("arbitrary",), vmem_limit_bytes=VMEM_LIMIT),
        name="final_norm",
    )(x, g)


def _block_diag(w):
    nb, bw, _ = w.shape
    eye = jnp.eye(nb, dtype=w.dtype)
    return (eye[:, None, :, None] * w[:, :, None, :]).reshape(nb * bw, nb * bw)


def kernel(x_prompt, x_sample, c_prompt, c_sample, state_rglru_h, state_rglru_conv, state_hgrn,
           mod_w, mod_b, norm1_g, norm2_g, w_in, conv_w, conv_b, rg_wa, rg_ba, rg_wx, rg_bx,
           rg_lambda, rg_out_g, hg_lb_logits, hg_norm_g, w_out, ffn_w1, ffn_w3, ffn_w2,
           router_w, moe_w1, moe_w3, moe_w2, final_g):
    n_b, seq, _ = x_prompt.shape
    n_s = x_sample.shape[0]
    tm_p = TM_FFN
    tiles_per_seq = seq // tm_p

    mod = _modulation(jnp.concatenate([c_prompt, c_sample], axis=0), mod_w, mod_b)
    lbs = _lower_bounds(hg_lb_logits)

    def row(v):
        return v.reshape(1, -1)

    w_in_b = w_in.astype(BF16)
    w_out_b = w_out.astype(BF16)
    dense_w = [w.astype(BF16)[:, None] for w in (ffn_w1, ffn_w3, ffn_w2)]
    moe_w = [w.astype(BF16) for w in (moe_w1, moe_w3, moe_w2)]
    conv_all = state_rglru_conv.reshape(DEPTH, n_s, (CONV_W - 1) * W_A)

    xp = x_prompt
    xs = x_sample.reshape(n_s, D_MODEL)
    outs = {k: [] for k in ("hp", "cp", "sp", "hs", "cs", "ss")}
    for l in range(DEPTH):
        mod_p = mod[l, :n_b].reshape(n_b, 1, 6 * D_MODEL)
        mod_s = mod[l, n_b:]
        pvec = jnp.concatenate([
            conv_w[l], row(conv_b[l]), row(rg_ba[l]), row(rg_bx[l]), row(rg_lambda[l]), row(rg_out_g[l]),
            row(lbs[l]), row(hg_norm_g[l]), jnp.zeros((_P_ROWS - 11, W_A), F32)], axis=0)
        bd_a, bd_x, half = _block_diag(rg_wa[l]), _block_diag(rg_wx[l]), W_A // 2
        wg = jnp.stack([
            jnp.concatenate([bd[hh * half:(hh + 1) * half, hh * half:(hh + 1) * half] for bd in (bd_a, bd_x)], axis=1)
            for hh in range(2)]).astype(BF16)

        xp, hp, tail, sp = _mix_prompt(xp, row(norm1_g[l]), mod_p, w_in_b, pvec, wg, w_out_b, l)
        outs["hp"].append(hp.reshape(n_b, W_A))
        outs["cp"].append(tail[:, SUBLANES - (CONV_W - 1):, :])
        outs["sp"].append(sp)

        u_s = _in_sample(xs, row(norm1_g[l]), mod_s, w_in_b, l)
        y_s, hs, cs, ss = _mix_sample(u_s, state_rglru_h, conv_all, state_hgrn, pvec, wg, l)
        xs = _out_sample(xs, y_s, mod_s, w_out_b, l)
        outs["hs"].append(hs)
        outs["cs"].append(cs.reshape(n_s, CONV_W - 1, W_A))
        outs["ss"].append(ss)

        if l % 2 == 0:
            w1, w3, w2 = dense_w
            router = None
        else:
            w1, w3, w2 = moe_w
            router = jnp.pad(router_w[l // 2], ((0, 0), (0, LANES - N_EXPERTS)))
        spec_p = lambda c: pl.BlockSpec((None, 1, D_MODEL), lambda i, e, j: (i // tiles_per_seq, 0, c))
        spec_s = lambda c: pl.BlockSpec((n_s, D_MODEL), lambda i, e, j: (0, c))
        if router is None:
            xp = _ffn(xp.reshape(n_b * seq, D_MODEL), row(norm2_g[l]), mod_p, spec_p, w1, w3, w2, l // 2, None,
                      tm=tm_p)
        else:
            xp = _moe_routed(xp.reshape(n_b * seq, D_MODEL), row(norm2_g[l]), mod_p, lambda tm: seq // tm,
                             router, w1, w3, w2, l // 2, row(final_g) if l == DEPTH - 1 else None)
        xp = xp.reshape(n_b, seq, D_MODEL)
        xs = _ffn(xs, row(norm2_g[l]), mod_s, spec_s, w1, w3, w2, l // 2, router, tm=n_s)

    y_prompt = xp
    y_sample = _final_norm(xs, row(final_g), tm=n_s).reshape(n_s, 1, D_MODEL)
    return (y_prompt, y_sample,
            jnp.stack(outs["hp"]), jnp.stack(outs["cp"]), jnp.stack(outs["sp"]),
            jnp.stack(outs["hs"]), jnp.stack(outs["cs"]), jnp.stack(outs["ss"]))
```

```python
import functools

import jax
import jax.numpy as jnp
from jax import lax
from jax.experimental import pallas as pl
from jax.experimental.pallas import tpu as pltpu

F32 = jnp.float32
BF16 = jnp.bfloat16
HIGHEST = lax.Precision.HIGHEST

D_MODEL = 1024
DEPTH = 4
W_A = 512
RG_BLOCKS = 8
CONV_W = 4
RG_C = 8.0
W_B = 512
HG_HEADS = 4
HG_D = 128
D_IN = 3072
D_FF = 2816
N_EXPERTS = 8
EPS = 1e-6
PAST_LEN = 16384

LANES = 128
SUBLANES = 8
VMEM_LIMIT = 56 * 1024 * 1024

TC_MIX = 1024
HG_CHUNK = 128
TM_FFN = 512
TF_FFN = 1408
TM_MOVE = 512
TM_GROUP = 512
BS_SAMPLE = 8
TN_MOD = 1536
TN_IN = 1024

_P_CONV_W, _P_CONV_B, _P_BA, _P_BX, _P_LAM, _P_RG_G, _P_LB, _P_HG_G = 0, 4, 5, 6, 7, 8, 9, 10
_P_ROWS = 16


def _silu(x):
    return x * jax.nn.sigmoid(x)


def _gelu_tanh(x):
    cdf = 0.5 * (1.0 + jnp.tanh(0.7978845608028654 * (x + 0.044715 * (x * x * x))))
    return x * cdf


def _rmsnorm(x, g):
    return x * lax.rsqrt(jnp.mean(x * x, axis=-1, keepdims=True) + EPS) * g


def _softplus(z):
    return jnp.maximum(z, 0.0) + jnp.log1p(jnp.exp(-jnp.abs(z)))


def _bdot(a, b):
    return jnp.dot(a.astype(BF16), b.astype(BF16), preferred_element_type=F32)


def _rglru_coeffs(xc, wg, p_ref):
    half = W_A // 2
    g0 = _bdot(xc[:, :half], wg[0])
    g1 = _bdot(xc[:, half:], wg[1])
    r = jax.nn.sigmoid(jnp.concatenate([g0[:, :half], g1[:, :half]], axis=1) + p_ref[_P_BA:_P_BA + 1, :])
    i = jax.nn.sigmoid(jnp.concatenate([g0[:, half:], g1[:, half:]], axis=1) + p_ref[_P_BX:_P_BX + 1, :])
    log_a = (-RG_C * _softplus(-p_ref[_P_LAM:_P_LAM + 1, :])) * r
    a = jnp.exp(log_a)
    one_minus_a2 = -jnp.tanh(log_a) * (a * a + 1.0)
    return a, one_minus_a2, i * xc


def _sqrt_nonneg(x):
    return jnp.where(x == 0.0, 0.0, x * lax.rsqrt(x))


def _group_roll(x, k):
    r, w = x.shape
    return pltpu.roll(x.reshape(r // SUBLANES, SUBLANES, w), k, 1).reshape(r, w)


def _scan_rows(a, bb, h0):
    r, w = a.shape
    sub = lax.broadcasted_iota(jnp.int32, (r, w), 0) & (SUBLANES - 1)
    k = 1
    while k < SUBLANES:
        keep = sub >= k
        bb = a * jnp.where(keep, _group_roll(bb, k), 0.0) + bb
        a = a * jnp.where(keep, _group_roll(a, k), 1.0)
        k *= 2
    outs = []
    carry = h0
    for g in range(r // SUBLANES):
        rows = slice(g * SUBLANES, (g + 1) * SUBLANES)
        hg = a[rows] * carry + bb[rows]
        outs.append(hg)
        carry = hg[SUBLANES - 1:SUBLANES, :]
    return jnp.concatenate(outs, axis=0)


def _head_rmsnorm_gate(o, gb, g_row):
    outs = []
    for h in range(HG_HEADS):
        hs = slice(h * HG_D, (h + 1) * HG_D)
        outs.append(_rmsnorm(o[:, hs], g_row[:, hs]))
    return jnp.concatenate(outs, axis=1) * _silu(gb)


def _mod_kernel(c_ref, w_ref, b_ref, o_ref):
    o_ref[...] = _bdot(_silu(c_ref[...]), w_ref[...]) + b_ref[...]


def _modulation(c_all, mod_w, mod_b):
    rows = c_all.shape[0]
    tn = TN_MOD
    n_out = mod_w.shape[-1]
    return pl.pallas_call(
        _mod_kernel,
        grid=(DEPTH, n_out // tn),
        in_specs=[
            pl.BlockSpec((rows, D_MODEL), lambda l, j: (0, 0)),
            pl.BlockSpec((None, D_MODEL, tn), lambda l, j: (l, 0, j)),
            pl.BlockSpec((None, 1, tn), lambda l, j: (l, 0, j)),
        ],
        out_specs=pl.BlockSpec((None, rows, tn), lambda l, j: (l, 0, j)),
        out_shape=jax.ShapeDtypeStruct((DEPTH, rows, n_out), F32),
        compiler_params=pltpu.CompilerParams(
            dimension_semantics=("arbitrary", "arbitrary"), vmem_limit_bytes=VMEM_LIMIT),
        name="adaln_mod",
    )(c_all, mod_w, mod_b.reshape(DEPTH, 1, n_out))


def _lb_kernel(l_ref, o_ref):
    x = l_ref[...]
    e = jnp.exp(x - jnp.max(x, axis=0, keepdims=True))
    p = e / jnp.sum(e, axis=0, keepdims=True)
    p0 = p[0:1, :]
    run = p0
    o_ref[0:1, :] = run - p0
    for r in range(1, DEPTH):
        run = run + p[r:r + 1, :]
        o_ref[r:r + 1, :] = run - p0


def _lower_bounds(hg_lb_logits):
    return pl.pallas_call(
        _lb_kernel,
        out_shape=jax.ShapeDtypeStruct(hg_lb_logits.shape, F32),
        name="hgrn_lower_bounds",
    )(hg_lb_logits)


def _hgrn_chunk(q, k, v, f, logf, st):
    C = q.shape[0]
    G = C // SUBLANES
    rc = lax.broadcasted_iota(jnp.int32, (C, C), 0)
    cc = lax.broadcasted_iota(jnp.int32, (C, C), 1)
    sr = lax.broadcasted_iota(jnp.int32, (C + 2 * G, C), 0)
    sc = lax.broadcasted_iota(jnp.int32, (C + 2 * G, C), 1)
    bound = jnp.where(sr < C, sr + 1,
                      jnp.where(sr < C + G, (sr - C) * SUBLANES, (sr - C - G + 1) * SUBLANES))
    hi = logf.astype(BF16)
    rest = logf - hi.astype(F32)
    mid = rest.astype(BF16)
    lo = (rest - mid.astype(F32)).astype(BF16)
    sums = jnp.dot((sc < bound).astype(BF16), jnp.concatenate([hi, mid, lo], axis=1), preferred_element_type=F32)
    sums = (sums[:, :W_B] + sums[:, W_B:2 * W_B]) + sums[:, 2 * W_B:]
    b, g_start, g_end = sums[:C], sums[C:C + G], sums[C + G:]
    row = lax.broadcasted_iota(jnp.int32, (C, W_B), 0)
    xor = rc ^ cc

    heads = [slice(h * HG_D, (h + 1) * HG_D) for h in range(HG_HEADS)]
    nt = (((1,), (1,)), ((), ()))
    qb, kb = q.astype(BF16), k.astype(BF16)
    att = [jnp.where(rc == cc, lax.dot_general(qb[:, hs], kb[:, hs], nt, preferred_element_type=F32), 0.0)
           for hs in heads]

    def add_level(x, m, lg):
        mask = ((xor >> lg) == 1) & ((rc & m) != 0)
        for h, hs in enumerate(heads):
            p = lax.dot_general(x[:, hs], x[:, hs], nt, preferred_element_type=F32)
            att[h] = jnp.where(mask, p, att[h])

    c = b - logf
    d = b
    m, lg = 1, 0
    while m < SUBLANES:
        odd = (row & m) != 0
        if m == 1:
            x = jnp.where(odd, q * f, k)
        else:
            x = jnp.where(odd, q, k) * jnp.exp(jnp.where(odd, b - c, d - b))
        add_level(x.astype(BF16), m, lg)
        c = jnp.where(odd, _group_roll(c, m), c)
        d = jnp.where(odd, d, _group_roll(d, SUBLANES - m))
        m, lg = 2 * m, lg + 1

    q8 = q * jnp.exp(b - c)
    k8 = k * jnp.exp(d - b)
    grow = lax.broadcasted_iota(jnp.int32, (G, W_B), 0)
    cg, dg = g_start, g_end
    mu = 1
    while mu < G:
        oddg = (grow & mu) != 0
        rg = jnp.exp(jnp.where(oddg, g_start - cg, dg - g_end))
        pieces = []
        for g in range(G):
            base = q8 if (g // mu) % 2 == 1 else k8
            pieces.append(base[g * SUBLANES:(g + 1) * SUBLANES] * rg[g:g + 1, :])
        add_level(jnp.concatenate(pieces, axis=0).astype(BF16), m, lg)
        cg = jnp.where(oddg, pltpu.roll(cg, mu, 0), cg)
        dg = jnp.where(oddg, dg, pltpu.roll(dg, G - mu, 0))
        mu, m, lg = 2 * mu, 2 * m, lg + 1

    b_last = b[C - 1:C, :]
    qe = (q * jnp.exp(b)).astype(BF16)
    kd = (k * jnp.exp(b_last - b)).astype(BF16)
    vb = v.astype(BF16)
    tn = (((0,), (0,)), ((), ()))
    outs, new_st = [], []
    for h, hs in enumerate(heads):
        o_inter = lax.dot_general(qe[:, hs], st[h].astype(BF16), nt, preferred_element_type=F32)
        o_intra = jnp.dot(att[h].astype(BF16), vb[:, hs], preferred_element_type=F32)
        outs.append(o_inter + o_intra)
        upd = lax.dot_general(vb[:, hs], kd[:, hs], tn, preferred_element_type=F32)
        new_st.append(jnp.exp(b_last[:, hs]) * st[h] + upd)
    return jnp.concatenate(outs, axis=1), new_st


def _mix_prompt_kernel(x_ref, g1_ref, sh_ref, sc_ref, gt_ref, win_ref, p_ref, wg_ref, wout_ref,
                       xo_ref, hn_ref, tail_ref, sn_ref,
                       hcar, ext, st_scr, *, hg_chunk):
    t = pl.program_id(1)
    tc = x_ref.shape[0]

    @pl.when(t == 0)
    def _():
        hcar[...] = jnp.zeros_like(hcar)
        ext[0:SUBLANES, :] = jnp.zeros((SUBLANES, W_A), F32)
        st_scr[...] = jnp.zeros_like(st_scr)

    x = x_ref[...]
    h = _rmsnorm(x, g1_ref[...]) * (1.0 + sc_ref[...]) + sh_ref[...]
    u = _bdot(h, win_ref[...])

    xa = u[:, 0:W_A]
    ga = u[:, W_A:2 * W_A]
    ext[SUBLANES:, :] = xa
    xc = p_ref[_P_CONV_B:_P_CONV_B + 1, :]
    for j in range(CONV_W):
        lo = SUBLANES - (CONV_W - 1) + j
        xc = xc + ext[lo:lo + tc, :] * p_ref[_P_CONV_W + j:_P_CONV_W + j + 1, :]
    tail = xa[tc - SUBLANES:, :]
    ext[0:SUBLANES, :] = tail
    tail_ref[...] = tail

    a, one_minus_a2, ix = _rglru_coeffs(xc, wg_ref, p_ref)
    row = lax.broadcasted_iota(jnp.int32, (tc, W_A), 0)
    first = (row == 0) & (t == 0)
    a = jnp.where(first, 0.0, a)
    bb = jnp.where(first, 1.0, _sqrt_nonneg(one_minus_a2)) * ix
    hseq = _scan_rows(a, bb, hcar[...])
    h_last = hseq[tc - 1:tc, :]
    hcar[...] = h_last
    hn_ref[...] = h_last
    y_a = _rmsnorm(hseq * _gelu_tanh(ga), p_ref[_P_RG_G:_P_RG_G + 1, :])

    lb = p_ref[_P_LB:_P_LB + 1, :]
    f = lb + (1.0 - lb) * jax.nn.sigmoid(u[:, 3 * W_A:4 * W_A])
    q = _silu(u[:, 2 * W_A:3 * W_A])
    kk = 1.0 - f
    logf = jnp.log(f)
    v = u[:, 4 * W_A:5 * W_A]
    st = [st_scr[hh] for hh in range(HG_HEADS)]
    outs = []
    for c0 in range(0, tc, hg_chunk):
        cs = slice(c0, c0 + hg_chunk)
        o_c, st = _hgrn_chunk(q[cs], kk[cs], v[cs], f[cs], logf[cs], st)
        outs.append(o_c)
    o = outs[0] if len(outs) == 1 else jnp.concatenate(outs, axis=0)
    for hh in range(HG_HEADS):
        st_scr[hh] = st[hh]
        sn_ref[hh] = st[hh].T
    y_b = _head_rmsnorm_gate(o, u[:, 5 * W_A:6 * W_A], p_ref[_P_HG_G:_P_HG_G + 1, :])

    y = _bdot(jnp.concatenate([y_a, y_b], axis=1), wout_ref[...])
    xo_ref[...] = x + gt_ref[...] * y


def _mix_prompt(x, g1, mod, w_in, pvec, wg, w_out, layer, *, tc=TC_MIX, hg_chunk=HG_CHUNK):
    bsz, seq, _ = x.shape
    const = lambda b, t: (0, 0)
    wconst = lambda b, t: (layer, 0, 0)
    kern = functools.partial(_mix_prompt_kernel, hg_chunk=hg_chunk)
    return pl.pallas_call(
        kern,
        grid=(bsz, seq // tc),
        in_specs=[
            pl.BlockSpec((None, tc, D_MODEL), lambda b, t: (b, t, 0)),
            pl.BlockSpec((1, D_MODEL), const),
            pl.BlockSpec((None, 1, D_MODEL), lambda b, t: (b, 0, 0)),
            pl.BlockSpec((None, 1, D_MODEL), lambda b, t: (b, 0, 1)),
            pl.BlockSpec((None, 1, D_MODEL), lambda b, t: (b, 0, 2)),
            pl.BlockSpec((None, D_MODEL, D_IN), wconst),
            pl.BlockSpec((_P_ROWS, W_A), const),
            pl.BlockSpec((2, W_A // 2, W_A), lambda b, t: (0, 0, 0)),
            pl.BlockSpec((None, D_MODEL, D_MODEL), wconst),
        ],
        out_specs=[
            pl.BlockSpec((None, tc, D_MODEL), lambda b, t: (b, t, 0)),
            pl.BlockSpec((None, 1, W_A), lambda b, t: (b, 0, 0)),
            pl.BlockSpec((None, SUBLANES, W_A), lambda b, t: (b, 0, 0)),
            pl.BlockSpec((None, HG_HEADS, HG_D, HG_D), lambda b, t: (b, 0, 0, 0)),
        ],
        out_shape=[
            jax.ShapeDtypeStruct(x.shape, F32),
            jax.ShapeDtypeStruct((bsz, 1, W_A), F32),
            jax.ShapeDtypeStruct((bsz, SUBLANES, W_A), F32),
            jax.ShapeDtypeStruct((bsz, HG_HEADS, HG_D, HG_D), F32),
        ],
        scratch_shapes=[
            pltpu.VMEM((1, W_A), F32),
            pltpu.VMEM((tc + SUBLANES, W_A), F32),
            pltpu.VMEM((HG_HEADS, HG_D, HG_D), F32),
        ],
        compiler_params=pltpu.CompilerParams(
            dimension_semantics=("arbitrary", "arbitrary"), vmem_limit_bytes=VMEM_LIMIT),
        name="mix_prompt",
    )(x, g1, mod, mod, mod, w_in, pvec, wg, w_out)


def _in_sample_kernel(x_ref, g1_ref, sh_ref, sc_ref, win_ref, u_ref):
    h = _rmsnorm(x_ref[...], g1_ref[...]) * (1.0 + sc_ref[...]) + sh_ref[...]
    u_ref[...] = _bdot(h, win_ref[...])


def _in_sample(x, g1, mod, w_in, layer):
    n = x.shape[0]
    tn = TN_IN
    return pl.pallas_call(
        _in_sample_kernel,
        grid=(D_IN // tn,),
        in_specs=[
            pl.BlockSpec((n, D_MODEL), lambda j: (0, 0)),
            pl.BlockSpec((1, D_MODEL), lambda j: (0, 0)),
            pl.BlockSpec((n, D_MODEL), lambda j: (0, 0)),
            pl.BlockSpec((n, D_MODEL), lambda j: (0, 1)),
            pl.BlockSpec((None, D_MODEL, tn), lambda j: (layer, 0, j)),
        ],
        out_specs=pl.BlockSpec((n, tn), lambda j: (0, j)),
        out_shape=jax.ShapeDtypeStruct((n, D_IN), F32),
        compiler_params=pltpu.CompilerParams(
            dimension_semantics=("arbitrary",), vmem_limit_bytes=VMEM_LIMIT),
        name="in_sample",
    )(x, g1, mod, mod, w_in)


def _mix_sample_kernel(u_ref, h0_ref, c0_ref, s0_ref, p_ref, wg_ref,
                       y_ref, hn_ref, cn_ref, sn_ref, o_scr):
    bs = u_ref.shape[0]
    u = u_ref[...]
    xa = u[:, 0:W_A]
    ga = u[:, W_A:2 * W_A]
    c0 = c0_ref[...]
    xc = p_ref[_P_CONV_B:_P_CONV_B + 1, :]
    for j in range(CONV_W - 1):
        xc = xc + c0[:, j * W_A:(j + 1) * W_A] * p_ref[_P_CONV_W + j:_P_CONV_W + j + 1, :]
    xc = xc + xa * p_ref[_P_CONV_W + CONV_W - 1:_P_CONV_W + CONV_W, :]
    cn_ref[:, 0:2 * W_A] = c0[:, W_A:]
    cn_ref[:, 2 * W_A:] = xa

    a, one_minus_a2, ix = _rglru_coeffs(xc, wg_ref, p_ref)
    hnew = _sqrt_nonneg(one_minus_a2) * ix + a * h0_ref[...]
    hn_ref[...] = hnew
    y_a = _rmsnorm(hnew * _gelu_tanh(ga), p_ref[_P_RG_G:_P_RG_G + 1, :])

    lb = p_ref[_P_LB:_P_LB + 1, :]
    f = lb + (1.0 - lb) * jax.nn.sigmoid(u[:, 3 * W_A:4 * W_A])
    q = _silu(u[:, 2 * W_A:3 * W_A])
    v = u[:, 4 * W_A:5 * W_A]
    eye = (lax.broadcasted_iota(jnp.int32, (bs, bs), 0) == lax.broadcasted_iota(jnp.int32, (bs, bs), 1)).astype(F32)
    fcols = lax.dot_general(f, eye, (((0,), (0,)), ((), ())), precision=HIGHEST, preferred_element_type=F32)
    qb = q.astype(BF16)
    for j in range(bs):
        for hh in range(HG_HEADS):
            hs = slice(hh * HG_D, (hh + 1) * HG_D)
            fb = jnp.broadcast_to(fcols[hh * HG_D:(hh + 1) * HG_D, j:j + 1], (HG_D, HG_D))
            s_new = fb * s0_ref[j, hh] + (1.0 - fb) * v[j:j + 1, hs]
            sn_ref[j, hh] = s_new
            q_rows = jnp.broadcast_to(qb[j:j + 1, hs], (SUBLANES, HG_D))
            o_rows = jnp.dot(q_rows, s_new.astype(BF16), preferred_element_type=F32)
            o_scr[j:j + 1, hs] = o_rows[0:1, :]
    y_b = _head_rmsnorm_gate(o_scr[...], u[:, 5 * W_A:6 * W_A], p_ref[_P_HG_G:_P_HG_G + 1, :])
    y_ref[:, 0:W_A] = y_a
    y_ref[:, W_A:] = y_b


def _mix_sample(u, h0, c0, s0, pvec, wg, layer, *, bs=BS_SAMPLE):
    n = u.shape[0]
    const = lambda i: (0, 0)
    return pl.pallas_call(
        _mix_sample_kernel,
        grid=(n // bs,),
        in_specs=[
            pl.BlockSpec((bs, D_IN), lambda i: (i, 0)),
            pl.BlockSpec((None, bs, W_A), lambda i: (layer, i, 0)),
            pl.BlockSpec((None, bs, (CONV_W - 1) * W_A), lambda i: (layer, i, 0)),
            pl.BlockSpec((None, bs, HG_HEADS, HG_D, HG_D), lambda i: (layer, i, 0, 0, 0)),
            pl.BlockSpec((_P_ROWS, W_A), const),
            pl.BlockSpec((2, W_A // 2, W_A), lambda i: (0, 0, 0)),
        ],
        out_specs=[
            pl.BlockSpec((bs, D_MODEL), lambda i: (i, 0)),
            pl.BlockSpec((bs, W_A), lambda i: (i, 0)),
            pl.BlockSpec((bs, (CONV_W - 1) * W_A), lambda i: (i, 0)),
            pl.BlockSpec((bs, HG_HEADS, HG_D, HG_D), lambda i: (i, 0, 0, 0)),
        ],
        out_shape=[
            jax.ShapeDtypeStruct((n, D_MODEL), F32),
            jax.ShapeDtypeStruct((n, W_A), F32),
            jax.ShapeDtypeStruct((n, (CONV_W - 1) * W_A), F32),
            jax.ShapeDtypeStruct((n, HG_HEADS, HG_D, HG_D), F32),
        ],
        scratch_shapes=[pltpu.VMEM((bs, W_B), F32)],
        compiler_params=pltpu.CompilerParams(
            dimension_semantics=("arbitrary",), vmem_limit_bytes=VMEM_LIMIT),
        name="mix_sample",
    )(u, h0, c0, s0, pvec, wg)


def _out_sample_kernel(x_ref, y_ref, gt_ref, wout_ref, xo_ref):
    xo_ref[...] = x_ref[...] + gt_ref[...] * _bdot(y_ref[...], wout_ref[...])


def _out_sample(x, y, mod, w_out, layer):
    n = x.shape[0]
    return pl.pallas_call(
        _out_sample_kernel,
        grid=(1,),
        in_specs=[
            pl.BlockSpec((n, D_MODEL), lambda i: (0, 0)),
            pl.BlockSpec((n, D_MODEL), lambda i: (0, 0)),
            pl.BlockSpec((n, D_MODEL), lambda i: (0, 2)),
            pl.BlockSpec((None, D_MODEL, D_MODEL), lambda i: (layer, 0, 0)),
        ],
        out_specs=pl.BlockSpec((n, D_MODEL), lambda i: (0, 0)),
        out_shape=jax.ShapeDtypeStruct((n, D_MODEL), F32),
        compiler_params=pltpu.CompilerParams(
            dimension_semantics=("arbitrary",), vmem_limit_bytes=VMEM_LIMIT),
        name="out_sample",
    )(x, y, mod, w_out)


def _top2(h, router):
    h_hi = h.astype(BF16)
    h_lo = (h - h_hi.astype(F32)).astype(BF16)
    r_hi = router.astype(BF16)
    r_lo = (router - r_hi.astype(F32)).astype(BF16)
    logits = (jnp.dot(h_hi, r_hi, preferred_element_type=F32) + jnp.dot(h_hi, r_lo, preferred_element_type=F32)
              ) + jnp.dot(h_lo, r_hi, preferred_element_type=F32)
    lane = lax.broadcasted_iota(jnp.int32, logits.shape, 1).astype(F32)
    neg = -jnp.inf
    lg = jnp.where(lane < N_EXPERTS, logits, neg)
    m1 = jnp.max(lg, axis=1, keepdims=True)
    i1 = jnp.min(jnp.where(lg == m1, lane, float(LANES)), axis=1, keepdims=True)
    lg2 = jnp.where(lane == i1, neg, lg)
    m2 = jnp.max(lg2, axis=1, keepdims=True)
    i2 = jnp.min(jnp.where(lg2 == m2, lane, float(LANES)), axis=1, keepdims=True)
    e2 = jnp.exp(m2 - m1)
    den = 1.0 + e2
    return lane, i1, i2, 1.0 / den, e2 / den


def _top2_gates(h, router):
    lane, i1, i2, w1, w2 = _top2(h, router)
    return jnp.where(lane == i1, w1, 0.0) + jnp.where(lane == i2, w2, 0.0)


def _ffn_kernel(*refs, moe):
    if moe:
        (x_ref, g_ref, sh_ref, sc_ref, gt_ref, r_ref, w1_ref, w3_ref, w2_ref,
         xo_ref, h_scr, acc_scr, gate_scr) = refs
    else:
        (x_ref, g_ref, sh_ref, sc_ref, gt_ref, w1_ref, w3_ref, w2_ref,
         xo_ref, h_scr, acc_scr) = refs
    e = pl.program_id(1)
    j = pl.program_id(2)

    @pl.when((e == 0) & (j == 0))
    def _():
        h = _rmsnorm(x_ref[...], g_ref[...]) * (1.0 + sc_ref[...]) + sh_ref[...]
        h_scr[...] = h.astype(BF16)
        acc_scr[...] = jnp.zeros_like(acc_scr)
        if moe:
            gate_scr[...] = _top2_gates(h, r_ref[...])

    h = h_scr[...]
    a = jnp.dot(h, w1_ref[...].astype(BF16), preferred_element_type=F32)
    b = jnp.dot(h, w3_ref[...].astype(BF16), preferred_element_type=F32)
    part = _bdot(_silu(a) * b, w2_ref[...])
    if moe:
        lane = lax.broadcasted_iota(jnp.int32, gate_scr.shape, 1)
        part = part * jnp.sum(jnp.where(lane == e, gate_scr[...], 0.0), axis=1, keepdims=True)
    acc_scr[...] += part

    @pl.when((e == pl.num_programs(1) - 1) & (j == pl.num_programs(2) - 1))
    def _():
        xo_ref[...] = x_ref[...] + gt_ref[...] * acc_scr[...]


def _ffn(x, g2, mod, mod_spec, w1, w3, w2, wl, router=None, *, tm, tf=TF_FFN):
    n = x.shape[0]
    n_exp = w1.shape[1]
    moe = router is not None
    in_specs = [
        pl.BlockSpec((tm, D_MODEL), lambda i, e, j: (i, 0)),
        pl.BlockSpec((1, D_MODEL), lambda i, e, j: (0, 0)),
        mod_spec(3), mod_spec(4), mod_spec(5),
    ]
    args = [x, g2, mod, mod, mod]
    scratch = [pltpu.VMEM((tm, D_MODEL), BF16), pltpu.VMEM((tm, D_MODEL), F32)]
    if moe:
        in_specs.append(pl.BlockSpec((D_MODEL, LANES), lambda i, e, j: (0, 0)))
        args.append(router)
        scratch.append(pltpu.VMEM((tm, LANES), F32))
    in_specs += [
        pl.BlockSpec((None, None, D_MODEL, tf), lambda i, e, j: (wl, e, 0, j)),
        pl.BlockSpec((None, None, D_MODEL, tf), lambda i, e, j: (wl, e, 0, j)),
        pl.BlockSpec((None, None, tf, D_MODEL), lambda i, e, j: (wl, e, j, 0)),
    ]
    args += [w1, w3, w2]
    return pl.pallas_call(
        functools.partial(_ffn_kernel, moe=moe),
        grid=(n // tm, n_exp, D_FF // tf),
        in_specs=in_specs,
        out_specs=pl.BlockSpec((tm, D_MODEL), lambda i, e, j: (i, 0)),
        out_shape=jax.ShapeDtypeStruct((n, D_MODEL), F32),
        scratch_shapes=scratch,
        compiler_params=pltpu.CompilerParams(
            dimension_semantics=("arbitrary", "arbitrary", "arbitrary"), vmem_limit_bytes=VMEM_LIMIT),
        name="ffn_moe" if moe else "ffn_dense",
    )(*args)


_R_I1, _R_I2, _R_W1, _R_W2, _R_RANK1, _R_RANK2 = 0, 1, 2, 3, 4, 5
_ROW_DMA_UNROLL = 8


def _route_kernel(x_ref, g_ref, sh_ref, sc_ref, r_ref, h_ref, route_ref, cnt_ref, run_scr):
    i = pl.program_id(0)
    tm = x_ref.shape[0]

    @pl.when(i == 0)
    def _():
        run_scr[...] = jnp.zeros_like(run_scr)

    h = _rmsnorm(x_ref[...], g_ref[...]) * (1.0 + sc_ref[...]) + sh_ref[...]
    h_ref[...] = h
    lane, i_hi, i_lo, w_hi, w_lo = _top2(h, r_ref[...])
    cnt = ((lane == i_hi) | (lane == i_lo)).astype(BF16)
    rr = lax.broadcasted_iota(jnp.int32, (tm, tm), 0)
    cc = lax.broadcasted_iota(jnp.int32, (tm, tm), 1)
    before = jnp.dot((rr > cc).astype(BF16), cnt, preferred_element_type=F32) + run_scr[...]
    rank_hi = jnp.sum(jnp.where(lane == i_hi, before, 0.0), axis=1, keepdims=True)
    rank_lo = jnp.sum(jnp.where(lane == i_lo, before, 0.0), axis=1, keepdims=True)
    run_scr[...] += jnp.sum(cnt.astype(F32), axis=0, keepdims=True)
    cnt_ref[...] = run_scr[...]
    rec = jnp.zeros(lane.shape, F32)
    for ln, val in ((_R_I1, i_hi), (_R_I2, i_lo), (_R_W1, w_hi), (_R_W2, w_lo),
                    (_R_RANK1, rank_hi), (_R_RANK2, rank_lo)):
        rec = jnp.where(lane == float(ln), val, rec)
    route_ref[...] = rec


def _route(x, g2, mod, mod_spec, router, *, tm):
    n = x.shape[0]
    return pl.pallas_call(
        _route_kernel,
        grid=(n // tm,),
        in_specs=[
            pl.BlockSpec((tm, D_MODEL), lambda i: (i, 0)),
            pl.BlockSpec((1, D_MODEL), lambda i: (0, 0)),
            mod_spec(3), mod_spec(4),
            pl.BlockSpec((D_MODEL, LANES), lambda i: (0, 0)),
        ],
        out_specs=[
            pl.BlockSpec((tm, D_MODEL), lambda i: (i, 0)),
            pl.BlockSpec((tm, LANES), lambda i: (i, 0)),
            pl.BlockSpec((1, LANES), lambda i: (0, 0)),
        ],
        out_shape=[
            jax.ShapeDtypeStruct((n, D_MODEL), F32),
            jax.ShapeDtypeStruct((n, LANES), F32),
            jax.ShapeDtypeStruct((1, LANES), F32),
        ],
        scratch_shapes=[pltpu.VMEM((1, LANES), F32)],
        compiler_params=pltpu.CompilerParams(
            dimension_semantics=("arbitrary",), vmem_limit_bytes=VMEM_LIMIT),
        name="moe_route",
    )(x, g2, mod, mod, router)


def _dispatch_kernel(pos_ref, h_ref, xs_in_ref, xs_ref, sem):
    del xs_in_ref
    rows = h_ref.shape[0]

    def row_copy(r, k):
        return pltpu.make_async_copy(
            h_ref.at[pl.ds(r, 1), :], xs_ref.at[pl.ds(pos_ref[2 * r + k], 1), :], sem)

    def start(r, carry):
        row_copy(r, 0).start(priority=0)
        row_copy(r, 1).start(priority=1)
        return carry

    def wait(r, carry):
        row_copy(r, 0).wait()
        row_copy(r, 1).wait()
        return carry

    lax.fori_loop(0, rows, start, 0, unroll=_ROW_DMA_UNROLL)
    lax.fori_loop(0, rows, wait, 0, unroll=_ROW_DMA_UNROLL)


def _dispatch(h, pos, n_rows, *, tm):
    n = h.shape[0]
    xs0 = jnp.zeros((n_rows, D_MODEL), F32)
    return pl.pallas_call(
        _dispatch_kernel,
        grid=(n // tm,),
        in_specs=[
            pl.BlockSpec((2 * tm,), lambda i: (i,), memory_space=pltpu.SMEM),
            pl.BlockSpec((tm, D_MODEL), lambda i: (i, 0)),
            pl.BlockSpec(memory_space=pl.ANY),
        ],
        out_specs=pl.BlockSpec(memory_space=pl.ANY),
        out_shape=jax.ShapeDtypeStruct((n_rows, D_MODEL), F32),
        scratch_shapes=[pltpu.SemaphoreType.DMA(())],
        input_output_aliases={2: 0},
        compiler_params=pltpu.CompilerParams(
            dimension_semantics=("arbitrary",), vmem_limit_bytes=VMEM_LIMIT),
        name="moe_dispatch",
    )(pos, h, xs0)


def _gffn_kernel(te_ref, nv_ref, xs_ref, w1_ref, w3_ref, w2_ref, ys_ref):
    del te_ref
    i = pl.program_id(0)
    j = pl.program_id(1)

    @pl.when(i < nv_ref[0])
    def _():
        @pl.when(j == 0)
        def _():
            ys_ref[...] = jnp.zeros_like(ys_ref)

        h = xs_ref[...].astype(BF16)
        a = jnp.dot(h, w1_ref[...].astype(BF16), preferred_element_type=F32)
        b = jnp.dot(h, w3_ref[...].astype(BF16), preferred_element_type=F32)
        ys_ref[...] += _bdot(_silu(a) * b, w2_ref[...])

    @pl.when((i >= nv_ref[0]) & (j == 0))
    def _():
        ys_ref[...] = jnp.zeros_like(ys_ref)


def _gffn(xs, tile_expert, n_valid, w1, w3, w2, wl, *, tm, tf=TF_FFN):
    n_rows = xs.shape[0]
    last_j = D_FF // tf - 1

    def row_idx(i, j, te, nv):
        return (jnp.minimum(i, nv[0] - 1), 0)

    def ff_idx(i, j, nv):
        return jnp.where(i < nv[0], j, last_j)

    grid_spec = pltpu.PrefetchScalarGridSpec(
        num_scalar_prefetch=2,
        grid=(n_rows // tm, D_FF // tf),
        in_specs=[
            pl.BlockSpec((tm, D_MODEL), row_idx),
            pl.BlockSpec((None, None, D_MODEL, tf), lambda i, j, te, nv: (wl, te[i], 0, ff_idx(i, j, nv))),
            pl.BlockSpec((None, None, D_MODEL, tf), lambda i, j, te, nv: (wl, te[i], 0, ff_idx(i, j, nv))),
            pl.BlockSpec((None, None, tf, D_MODEL), lambda i, j, te, nv: (wl, te[i], ff_idx(i, j, nv), 0)),
        ],
        out_specs=pl.BlockSpec((tm, D_MODEL), lambda i, j, te, nv: (i, 0)),
    )
    return pl.pallas_call(
        _gffn_kernel,
        grid_spec=grid_spec,
        out_shape=jax.ShapeDtypeStruct((n_rows, D_MODEL), F32),
        compiler_params=pltpu.CompilerParams(
            dimension_semantics=("arbitrary", "arbitrary"), vmem_limit_bytes=VMEM_LIMIT),
        name="moe_grouped_ffn",
    )(tile_expert, n_valid, xs, w1, w3, w2)


def _combine_kernel(pos_ref, x_ref, gt_ref, route_ref, fg_ref, ys_ref, xo_ref, y0_scr, y1_scr, sem, *, final_norm):
    rows = x_ref.shape[0]

    def row_copy(r, k):
        buf = y0_scr if k == 0 else y1_scr
        return pltpu.make_async_copy(
            ys_ref.at[pl.ds(pos_ref[2 * r + k], 1), :], buf.at[pl.ds(r, 1), :], sem)

    def start(r, carry):
        row_copy(r, 0).start(priority=0)
        row_copy(r, 1).start(priority=1)
        return carry

    def wait(r, carry):
        row_copy(r, 0).wait()
        row_copy(r, 1).wait()
        return carry

    lax.fori_loop(0, rows, start, 0, unroll=_ROW_DMA_UNROLL)
    lax.fori_loop(0, rows, wait, 0, unroll=_ROW_DMA_UNROLL)
    route = route_ref[...]
    fo = route[:, _R_W1:_R_W1 + 1] * y0_scr[...] + route[:, _R_W2:_R_W2 + 1] * y1_scr[...]
    x_new = x_ref[...] + gt_ref[...] * fo
    xo_ref[...] = _rmsnorm(x_new, fg_ref[...]) if final_norm else x_new


def _combine(x, mod, mod_spec, route, pos, ys, final_g, *, tm):
    n = x.shape[0]
    final_norm = final_g is not None
    fg = final_g if final_norm else jnp.ones((1, D_MODEL), F32)
    return pl.pallas_call(
        functools.partial(_combine_kernel, final_norm=final_norm),
        grid=(n // tm,),
        in_specs=[
            pl.BlockSpec((2 * tm,), lambda i: (i,), memory_space=pltpu.SMEM),
            pl.BlockSpec((tm, D_MODEL), lambda i: (i, 0)),
            mod_spec(5),
            pl.BlockSpec((tm, LANES), lambda i: (i, 0)),
            pl.BlockSpec((1, D_MODEL), lambda i: (0, 0)),
            pl.BlockSpec(memory_space=pl.ANY),
        ],
        out_specs=pl.BlockSpec((tm, D_MODEL), lambda i: (i, 0)),
        out_shape=jax.ShapeDtypeStruct((n, D_MODEL), F32),
        scratch_shapes=[pltpu.VMEM((tm, D_MODEL), F32), pltpu.VMEM((tm, D_MODEL), F32),
                        pltpu.SemaphoreType.DMA(())],
        compiler_params=pltpu.CompilerParams(
            dimension_semantics=("arbitrary",), vmem_limit_bytes=VMEM_LIMIT),
        name="moe_combine",
    )(pos, x, mod, route, fg, ys)


def _moe_routed(x, g2, mod, tiles_per_seq_of, router, w1, w3, w2, wl, final_g=None, *,
                tm_route=TM_FFN, tm_move=TM_MOVE, tm_group=TM_GROUP):
    n = x.shape[0]

    def spec1(tm):
        return lambda c: pl.BlockSpec((None, 1, D_MODEL), lambda i: (i // tiles_per_seq_of(tm), 0, c))

    h, route, counts = _route(x, g2, mod, spec1(tm_route), router, tm=tm_route)
    cnt = counts[0, :N_EXPERTS].astype(jnp.int32)
    tiles_e = (cnt + tm_group - 1) // tm_group
    tile_start = jnp.cumsum(tiles_e) - tiles_e
    n_valid = jnp.sum(tiles_e)
    n_tiles = (2 * n) // tm_group + N_EXPERTS
    tile_ids = jnp.minimum(jnp.arange(n_tiles, dtype=jnp.int32), n_valid - 1)
    tile_expert = (jnp.sum(tile_ids[:, None] >= tile_start[None, :], axis=1) - 1).astype(jnp.int32)
    idx = route[:, _R_I1:_R_I2 + 1].astype(jnp.int32)
    rank = route[:, _R_RANK1:_R_RANK2 + 1].astype(jnp.int32)
    first_row = jnp.sum(jnp.where(idx[..., None] == jnp.arange(N_EXPERTS, dtype=jnp.int32),
                                  tile_start * tm_group, 0), axis=-1)
    pos = (first_row + rank).reshape(2 * n)

    xs = _dispatch(h, pos, n_tiles * tm_group, tm=tm_move)
    ys = _gffn(xs, tile_expert, n_valid.reshape(1).astype(jnp.int32), w1, w3, w2, wl, tm=tm_group)
    return _combine(x, mod, spec1(tm_move), route, pos, ys, final_g, tm=tm_move)


def _final_norm_kernel(x_ref, g_ref, o_ref):
    o_ref[...] = _rmsnorm(x_ref[...], g_ref[...])


def _final_norm(x, g, *, tm):
    n = x.shape[0]
    return pl.pallas_call(
        _final_norm_kernel,
        grid=(n // tm,),
        in_specs=[pl.BlockSpec((tm, D_MODEL), lambda i: (i, 0)), pl.BlockSpec((1, D_MODEL), lambda i: (0, 0))],
        out_specs=pl.BlockSpec((tm, D_MODEL), lambda i: (i, 0)),
        out_shape=jax.ShapeDtypeStruct((n, D_MODEL), F32),
        compiler_params=pltpu.CompilerParams(
            dimension_semantics=("arbitrary",), vmem_limit_bytes=VMEM_LIMIT),
        name="final_norm",
    )(x, g)


def _block_diag(w):
    nb, bw, _ = w.shape
    eye = jnp.eye(nb, dtype=w.dtype)
    return (eye[:, None, :, None] * w[:, :, None, :]).reshape(nb * bw, nb * bw)


def kernel(x_prompt, x_sample, c_prompt, c_sample, state_rglru_h, state_rglru_conv, state_hgrn,
           mod_w, mod_b, norm1_g, norm2_g, w_in, conv_w, conv_b, rg_wa, rg_ba, rg_wx, rg_bx,
           rg_lambda, rg_out_g, hg_lb_logits, hg_norm_g, w_out, ffn_w1, ffn_w3, ffn_w2,
           router_w, moe_w1, moe_w3, moe_w2, final_g):
    n_b, seq, _ = x_prompt.shape
    n_s = x_sample.shape[0]
    tm_p = TM_FFN
    tiles_per_seq = seq // tm_p

    mod = _modulation(jnp.concatenate([c_prompt, c_sample], axis=0), mod_w, mod_b)
    lbs = _lower_bounds(hg_lb_logits)

    def row(v):
        return v.reshape(1, -1)

    w_in_b = w_in.astype(BF16)
    w_out_b = w_out.astype(BF16)
    dense_w = [w.astype(BF16)[:, None] for w in (ffn_w1, ffn_w3, ffn_w2)]
    moe_w = [w.astype(BF16) for w in (moe_w1, moe_w3, moe_w2)]
    conv_all = state_rglru_conv.reshape(DEPTH, n_s, (CONV_W - 1) * W_A)

    xp = x_prompt
    xs = x_sample.reshape(n_s, D_MODEL)
    outs = {k: [] for k in ("hp", "cp", "sp", "hs", "cs", "ss")}
    for l in range(DEPTH):
        mod_p = mod[l, :n_b].reshape(n_b, 1, 6 * D_MODEL)
        mod_s = mod[l, n_b:]
        pvec = jnp.concatenate([
            conv_w[l], row(conv_b[l]), row(rg_ba[l]), row(rg_bx[l]), row(rg_lambda[l]), row(rg_out_g[l]),
            row(lbs[l]), row(hg_norm_g[l]), jnp.zeros((_P_ROWS - 11, W_A), F32)], axis=0)
        bd_a, bd_x, half = _block_diag(rg_wa[l]), _block_diag(rg_wx[l]), W_A // 2
        wg = jnp.stack([
            jnp.concatenate([bd[hh * half:(hh + 1) * half, hh * half:(hh + 1) * half] for bd in (bd_a, bd_x)], axis=1)
            for hh in range(2)]).astype(BF16)

        xp, hp, tail, sp = _mix_prompt(xp, row(norm1_g[l]), mod_p, w_in_b, pvec, wg, w_out_b, l)
        outs["hp"].append(hp.reshape(n_b, W_A))
        outs["cp"].append(tail[:, SUBLANES - (CONV_W - 1):, :])
        outs["sp"].append(sp)

        u_s = _in_sample(xs, row(norm1_g[l]), mod_s, w_in_b, l)
        y_s, hs, cs, ss = _mix_sample(u_s, state_rglru_h, conv_all, state_hgrn, pvec, wg, l)
        xs = _out_sample(xs, y_s, mod_s, w_out_b, l)
        outs["hs"].append(hs)
        outs["cs"].append(cs.reshape(n_s, CONV_W - 1, W_A))
        outs["ss"].append(ss)

        if l % 2 == 0:
            w1, w3, w2 = dense_w
            router = None
        else:
            w1, w3, w2 = moe_w
            router = jnp.pad(router_w[l // 2], ((0, 0), (0, LANES - N_EXPERTS)))
        spec_p = lambda c: pl.BlockSpec((None, 1, D_MODEL), lambda i, e, j: (i // tiles_per_seq, 0, c))
        spec_s = lambda c: pl.BlockSpec((n_s, D_MODEL), lambda i, e, j: (0, c))
        if router is None:
            xp = _ffn(xp.reshape(n_b * seq, D_MODEL), row(norm2_g[l]), mod_p, spec_p, w1, w3, w2, l // 2, None,
                      tm=tm_p)
        else:
            xp = _moe_routed(xp.reshape(n_b * seq, D_MODEL), row(norm2_g[l]), mod_p, lambda tm: seq // tm,
                             router, w1, w3, w2, l // 2, row(final_g) if l == DEPTH - 1 else None)
        xp = xp.reshape(n_b, seq, D_MODEL)
        xs = _ffn(xs, row(norm2_g[l]), mod_s, spec_s, w1, w3, w2, l // 2, router, tm=n_s)

    y_prompt = xp
    y_sample = _final_norm(xs, row(final_g), tm=n_s).reshape(n_s, 1, D_MODEL)
    return (y_prompt, y_sample,
            jnp.stack(outs["hp"]), jnp.stack(outs["cp"]), jnp.stack(outs["sp"]),
            jnp.stack(outs["hs"]), jnp.stack(outs["cs"]), jnp.stack(outs["ss"]))
```
